```python
import math
import jax
import jax.numpy as jnp
from jax import lax
import numpy as np

D_MODEL = 1024
BATCH = 16
SEQ = 2048
DEPTH = 4
DEC_BATCH = 128
DEC_SEQ = 8
PAST_LEN = 8192
PAGE_SIZE = 128

N_MIXERS = 3
N_BUCKETS = 32
REL_MAX_DIST = 2048
DIFF_DH = 64
DIFF_HEADS = D_MODEL // (2 * DIFF_DH)
DIFF_KV_HEADS = 2
DIL_PATTERNS = ((128, 1), (512, 4), (2048, 16))
DIL_HEADS = 8
DIL_DH = 64
REL_HEADS = 8
MLA_HEADS = 16
MLA_NOPE = 64
MLA_ROPE = 32
MLA_V = 64
MLA_Q_RANK = 384
MLA_KV_RANK = 256
ROPE_BASE = 10000.0
MOE_GROUPS = 4
MOE_EXPERTS_PER_GROUP = 8
MOE_EXPERTS = MOE_GROUPS * MOE_EXPERTS_PER_GROUP
MOE_TOPK = 2
MOE_FF = 512
MOE_BLOCK = 128
QBLOCK = 128
LN_EPS = 1e-5
NEG_INF = -1e30
DEEPNORM_ALPHA = (2 * DEPTH) ** 0.25
DEEPNORM_BETA = (8 * DEPTH) ** -0.25
N_DIFF_LAYERS = len(range(0, DEPTH, N_MIXERS))
N_DIL_LAYERS = len(range(1, DEPTH, N_MIXERS))
N_MLA_LAYERS = len(range(2, DEPTH, N_MIXERS))

kernel_name = 'hybrid_diffattn_dilated_mla_hmoe_step'


def _layernorm(x, g, b):
    xf = x.astype(jnp.float32)
    mu = jnp.mean(xf, -1, keepdims=True)
    var = jnp.mean(jnp.square(xf - mu), -1, keepdims=True)
    return ((xf - mu) * lax.rsqrt(var + LN_EPS) * g + b).astype(x.dtype)


def _rmsnorm(x, g):
    xf = x.astype(jnp.float32)
    return (xf * lax.rsqrt(jnp.mean(xf * xf, -1, keepdims=True) + LN_EPS) * g).astype(x.dtype)


def _rel_bucket(dist):
    max_exact = N_BUCKETS // 2
    n = jnp.maximum(dist, 0)
    nf = jnp.maximum(n, 1).astype(jnp.float32)
    large = max_exact + (jnp.log(nf / max_exact) / math.log(REL_MAX_DIST / max_exact)
                         * (N_BUCKETS - max_exact)).astype(jnp.int32)
    return jnp.where(n < max_exact, n, jnp.minimum(large, N_BUCKETS - 1))


def _rel_bias(rel_bias, dist):
    return jnp.moveaxis(rel_bias[_rel_bucket(dist)].astype(jnp.float32), -1, 0)


def _probs(q, k, scale, bias, mask):
    s = jnp.einsum('qgrd,kgd->grqk', q, k).astype(jnp.float32) * scale + bias
    return jax.nn.softmax(jnp.where(mask, s, NEG_INF), axis=-1)


def _softmax_lse(s, mask):
    s = jnp.where(mask, s, NEG_INF)
    m = jnp.max(s, -1, keepdims=True)
    p = jnp.exp(s - m)
    l = jnp.sum(p, -1, keepdims=True)
    return p / l, (m + jnp.log(l))[..., 0]


def _rope(x, pos):
    half = MLA_ROPE // 2
    inv = ROPE_BASE ** (-jnp.arange(half, dtype=jnp.float32) / half)
    ang = pos.astype(jnp.float32)[:, None] * inv[None, :]
    ang = ang.reshape((ang.shape[0],) + (1,) * (x.ndim - 3) + (half,))
    cos, sin = jnp.cos(ang), jnp.sin(ang)
    x1, x2 = x[..., :half].astype(jnp.float32), x[..., half:].astype(jnp.float32)
    return jnp.concatenate([x1 * cos - x2 * sin, x1 * sin + x2 * cos], -1).astype(x.dtype)


def _ada(c, ada_w, ada_b):
    m = jax.nn.silu(c) @ ada_w + ada_b
    return jnp.split(m[:, None, :], 6, axis=-1)


def _diff_proj(h, w_in):
    proj = h @ w_in
    lead = proj.shape[:-1]
    nq = DIFF_HEADS * 2 * DIFF_DH
    nk = DIFF_KV_HEADS * 2 * DIFF_DH
    q = proj[..., :nq].reshape(lead + (DIFF_HEADS, 2 * DIFF_DH))
    k = proj[..., nq:nq + nk].reshape(lead + (DIFF_KV_HEADS, 2 * DIFF_DH))
    v = proj[..., nq + nk:].reshape(lead + (DIFF_KV_HEADS, 2 * DIFF_DH))
    return q, k, v


def _diff_lambda(lam_p, layer_idx):
    lam_init = 0.8 - 0.6 * math.exp(-0.3 * layer_idx)
    lp = lam_p.astype(jnp.float32)
    lam = jnp.exp(jnp.sum(lp[0] * lp[1])) - jnp.exp(jnp.sum(lp[2] * lp[3])) + lam_init
    return lam, lam_init


def _diff_core(q, k, v, qpos, kpos, lam, rel_bias):
    nq, nk = q.shape[0], k.shape[0]
    r = DIFF_HEADS // DIFF_KV_HEADS
    dist = qpos[:, None] - kpos[None, :]
    mask = dist >= 0
    bias = _rel_bias(rel_bias, dist).reshape(DIFF_KV_HEADS, r, nq, nk)
    qg = q.reshape(nq, DIFF_KV_HEADS, r, 2 * DIFF_DH)
    scale = DIFF_DH ** -0.5
    p1 = _probs(qg[..., :DIFF_DH], k[..., :DIFF_DH], scale, bias, mask)
    p2 = _probs(qg[..., DIFF_DH:], k[..., DIFF_DH:], scale, bias, mask)
    o = jnp.einsum('grqk,kgd->qgrd', p1 - lam * p2, v.astype(jnp.float32))
    return o.reshape(nq, DIFF_HEADS, 2 * DIFF_DH).astype(v.dtype)


def _diff_finish(o, subln, lam_init, w_out):
    o = _rmsnorm(o, subln) * (1.0 - lam_init)
    return o.reshape(o.shape[:-2] + (DIFF_HEADS * 2 * DIFF_DH,)) @ w_out


def _diff_prompt(h, w_in, lam_p, subln, w_out, rel_bias, layer_idx):
    b, s, _ = h.shape
    q, k, v = _diff_proj(h, w_in)
    lam, lam_init = _diff_lambda(lam_p, layer_idx)
    nb = s // QBLOCK
    kpos = jnp.arange(s)
    qblk = q.reshape(b, nb, QBLOCK, DIFF_HEADS, 2 * DIFF_DH).swapaxes(0, 1)

    def block(args):
        qb_, start = args
        qpos = start + jnp.arange(QBLOCK)
        return jax.vmap(lambda qq, kk, vv: _diff_core(qq, kk, vv, qpos, kpos, lam, rel_bias))(qb_, k, v)

    o = lax.map(block, (qblk, jnp.arange(nb) * QBLOCK))
    o = o.swapaxes(0, 1).reshape(b, s, DIFF_HEADS, 2 * DIFF_DH)
    return _diff_finish(o, subln, lam_init, w_out), k, v


def _diff_sample(h, cache_k, cache_v, slot, page_table, w_in, lam_p, subln, w_out, rel_bias, layer_idx):
    db, t, _ = h.shape
    past = page_table.shape[1] * cache_k.shape[2]
    q, k, v = _diff_proj(h, w_in)
    lam, lam_init = _diff_lambda(lam_p, layer_idx)
    qpos = past + jnp.arange(t)
    kpos = jnp.arange(past + t)

    def one(args):
        qq, kk, vv, pages = args
        pk = cache_k[slot, pages].reshape(past, DIFF_KV_HEADS, 2 * DIFF_DH).astype(kk.dtype)
        pv = cache_v[slot, pages].reshape(past, DIFF_KV_HEADS, 2 * DIFF_DH).astype(vv.dtype)
        return _diff_core(qq, jnp.concatenate([pk, kk], 0), jnp.concatenate([pv, vv], 0),
                          qpos, kpos, lam, rel_bias)

    o = lax.map(one, (q, k, v, page_table))
    return _diff_finish(o, subln, lam_init, w_out), k, v


def _dil_proj(h, w_in):
    proj = h @ w_in
    return proj.reshape(proj.shape[:-1] + (len(DIL_PATTERNS), 3, DIL_HEADS, DIL_DH))


def _dil_merge(outs, lses):
    w = jax.nn.softmax(jnp.stack(lses, 0), axis=0)
    return jnp.einsum('gbsh,gbshe->bshe', w, jnp.stack(outs, 0))


def _dil_group_prompt(q, k, v, window, dil, rel_bias):
    b, s, nh, e = q.shape
    L = s // dil
    J = window // dil
    qb = math.gcd(L, QBLOCK)
    nb = L // qb

    def strided(a):
        return a.reshape(b, L, dil, nh, e).transpose(0, 2, 1, 3, 4)

    qs = strided(q).reshape(b, dil, nb, qb, nh, e)
    pad = ((0, 0), (0, 0), (J, 0), (0, 0), (0, 0))
    kp = jnp.pad(strided(k), pad)
    vp = jnp.pad(strided(v), pad)
    m_loc = jnp.arange(qb + J)
    idx = (jnp.arange(nb) * qb)[:, None] + m_loc[None, :]
    kw = kp[:, :, idx]
    vw = vp[:, :, idx]
    delta = jnp.arange(qb)[:, None] + J - m_loc[None, :]
    mask = ((delta >= 0) & (delta <= J))[None] & ((idx - J) >= 0)[:, None, :]
    bias = _rel_bias(rel_bias, jnp.clip(delta, 0, J) * dil)
    sc = jnp.einsum('bcnqhe,bcnkhe->bcnhqk', qs, kw).astype(jnp.float32) * (DIL_DH ** -0.5) + bias[None, None, None]
    p, lse = _softmax_lse(sc, mask[None, None, :, None])
    o = jnp.einsum('bcnhqk,bcnkhe->bcnqhe', p, vw.astype(jnp.float32))
    o = o.reshape(b, dil, L, nh, e).transpose(0, 2, 1, 3, 4).reshape(b, s, nh, e)
    lse = lse.transpose(0, 1, 2, 4, 3).reshape(b, dil, L, nh).transpose(0, 2, 1, 3).reshape(b, s, nh)
    return o, lse


def _dil_group_sample(q, k, v, buf, window, dil, past, rel_bias):
    t = q.shape[1]
    wb = buf.shape[1]
    J = window // dil
    k_all = jnp.concatenate([buf[:, :, 0].astype(k.dtype), k], 1)
    v_all = jnp.concatenate([buf[:, :, 1].astype(v.dtype), v], 1)
    qpos = past + jnp.arange(t)
    steps = jnp.arange(J + 1) * dil
    kpos = qpos[:, None] - steps[None, :]
    valid = kpos >= 0
    idx = jnp.clip(kpos - (past - wb), 0, wb + t - 1)
    kg = k_all[:, idx]
    vg = v_all[:, idx]
    bias = _rel_bias(rel_bias, steps)
    sc = jnp.einsum('bthe,btjhe->bhtj', q, kg).astype(jnp.float32) * (DIL_DH ** -0.5) + bias[None, :, None, :]
    p, lse = _softmax_lse(sc, valid[None, None])
    o = jnp.einsum('bhtj,btjhe->bthe', p, vg.astype(jnp.float32))
    new_buf = jnp.concatenate([buf, jnp.stack([k, v], 2).astype(buf.dtype)], 1)[:, -wb:]
    return o, lse.transpose(0, 2, 1), new_buf


def _dil_prompt(h, w_in, w_out, rel_bias):
    b, s, _ = h.shape
    proj = _dil_proj(h, w_in)
    outs, lses, bufs = [], [], []
    for g, (win, dil) in enumerate(DIL_PATTERNS):
        q, k, v = proj[:, :, g, 0], proj[:, :, g, 1], proj[:, :, g, 2]
        o, lse = _dil_group_prompt(q, k, v, win, dil, rel_bias)
        outs.append(o)
        lses.append(lse)
        bufs.append(jnp.stack([k, v], 2)[:, s - min(win, s):])
    o = _dil_merge(outs, lses).astype(h.dtype)
    return o.reshape(b, s, DIL_HEADS * DIL_DH) @ w_out, bufs


def _dil_sample(h, bufs, past, w_in, w_out, rel_bias):
    db, t, _ = h.shape
    proj = _dil_proj(h, w_in)
    outs, lses, new = [], [], []
    for g, (win, dil) in enumerate(DIL_PATTERNS):
        o, lse, nbuf = _dil_group_sample(proj[:, :, g, 0], proj[:, :, g, 1], proj[:, :, g, 2],
                                         bufs[g], win, dil, past, rel_bias)
        outs.append(o)
        lses.append(lse)
        new.append(nbuf)
    o = _dil_merge(outs, lses).astype(h.dtype)
    return o.reshape(db, t, DIL_HEADS * DIL_DH) @ w_out, new


def _mla_proj(h, pos, w_in, q_norm, kv_norm, w_uq):
    proj = h @ w_in
    r0, r1 = MLA_Q_RANK, MLA_Q_RANK + MLA_KV_RANK
    cq = _rmsnorm(proj[..., :r0], q_norm)
    ckv = _rmsnorm(proj[..., r0:r1], kv_norm)
    krope = _rope(proj[..., r1:], pos)
    q = (cq @ w_uq).reshape(h.shape[:-1] + (MLA_HEADS, MLA_NOPE + MLA_ROPE))
    return q[..., :MLA_NOPE], _rope(q[..., MLA_NOPE:], pos), ckv, krope


def _mla_prompt(h, w_in, q_norm, kv_norm, w_uq, w_uk, w_uv, w_out):
    b, s, _ = h.shape
    pos = jnp.arange(s)
    q_nope, q_rope, ckv, krope = _mla_proj(h, pos, w_in, q_norm, kv_norm, w_uq)
    k_nope = jnp.einsum('bsr,rhn->bshn', ckv, w_uk)
    v = jnp.einsum('bsr,rhe->bshe', ckv, w_uv)
    q = jnp.concatenate([q_nope, q_rope], -1)
    k = jnp.concatenate([k_nope, jnp.broadcast_to(krope[:, :, None, :], (b, s, MLA_HEADS, MLA_ROPE))], -1)
    nb = s // QBLOCK
    qblk = q.reshape(b, nb, QBLOCK, MLA_HEADS, MLA_NOPE + MLA_ROPE).swapaxes(0, 1)
    scale = (MLA_NOPE + MLA_ROPE) ** -0.5

    def block(args):
        qb_, start = args
        mask = (start + jnp.arange(QBLOCK))[:, None] >= pos[None, :]

        def one(qq, kk, vv):
            p = _probs(qq[:, :, None], kk, scale, 0.0, mask)
            return jnp.einsum('grqk,kge->qge', p, vv.astype(jnp.float32)).astype(vv.dtype)

        return jax.vmap(one)(qb_, k, v)

    o = lax.map(block, (qblk, jnp.arange(nb) * QBLOCK))
    o = o.swapaxes(0, 1).reshape(b, s, MLA_HEADS * MLA_V)
    return o @ w_out, ckv, krope


def _mla_sample(h, cache_ckv, cache_krope, slot, page_table, w_in, q_norm, kv_norm, w_uq, w_uk, w_uv, w_out):
    db, t, _ = h.shape
    past = page_table.shape[1] * cache_ckv.shape[2]
    pos = past + jnp.arange(t)
    q_nope, q_rope, ckv, krope = _mla_proj(h, pos, w_in, q_norm, kv_norm, w_uq)
    q_abs = jnp.concatenate([jnp.einsum('bthn,rhn->bthr', q_nope, w_uk), q_rope], -1)
    mask = pos[:, None] >= jnp.arange(past + t)[None, :]
    scale = (MLA_NOPE + MLA_ROPE) ** -0.5

    def one(args):
        qq, cc, rr, pages = args
        lat = jnp.concatenate([cache_ckv[slot, pages].reshape(past, MLA_KV_RANK).astype(cc.dtype), cc], 0)
        rot = jnp.concatenate([cache_krope[slot, pages].reshape(past, MLA_ROPE).astype(rr.dtype), rr], 0)
        kk = jnp.concatenate([lat, rot], -1)[:, None, :]
        p = _probs(qq[:, None], kk, scale, 0.0, mask)
        return jnp.einsum('grqk,kc->qrc', p, lat.astype(jnp.float32)).astype(cc.dtype)

    o_lat = lax.map(one, (q_abs, ckv, krope, page_table))
    o = jnp.einsum('bthr,rhe->bthe', o_lat, w_uv).reshape(db, t, MLA_HEADS * MLA_V)
    return o @ w_out, ckv, krope


def _expert_dispatch(x, eid, gate, w1, w3, w2):
    n, d = x.shape
    n_exp = w1.shape[0]
    a = n * MOE_TOPK
    flat_e = eid.reshape(a)
    flat_tok = jnp.arange(a, dtype=jnp.int32) // MOE_TOPK
    flat_g = gate.reshape(a)
    order = jnp.argsort(flat_e)
    se = flat_e[order]
    counts = jnp.bincount(flat_e, length=n_exp)
    padded = (counts + MOE_BLOCK - 1) // MOE_BLOCK * MOE_BLOCK
    start = jnp.cumsum(counts) - counts
    pend = jnp.cumsum(padded)
    dest = (pend - padded)[se] + jnp.arange(a) - start[se]
    n_blocks = -(-a // MOE_BLOCK) + n_exp
    n_slots = n_blocks * MOE_BLOCK
    slot_tok = jnp.full((n_slots,), n, jnp.int32).at[dest].set(flat_tok[order])
    slot_g = jnp.zeros((n_slots,), jnp.float32).at[dest].set(flat_g[order])
    block_e = jnp.minimum(jnp.searchsorted(pend, jnp.arange(n_blocks) * MOE_BLOCK, side='right'), n_exp - 1)
    xb = jnp.concatenate([x, jnp.zeros((1, d), x.dtype)], 0)[slot_tok].reshape(n_blocks, MOE_BLOCK, d)

    def run(args):
        xx, e = args
        return (jax.nn.silu(xx @ w1[e]) * (xx @ w3[e])) @ w2[e]

    yb = lax.map(run, (xb, block_e)).reshape(n_slots, d)
    y = jnp.zeros((n + 1, d), jnp.float32).at[slot_tok].add(yb.astype(jnp.float32) * slot_g[:, None])
    return y[:n].astype(x.dtype)


def _moe(h, wg, bg, we, be, w1, w3, w2):
    shp = h.shape
    x = h.reshape(-1, shp[-1])
    n = x.shape[0]
    lg = (x @ wg).astype(jnp.float32) + bg
    gsel = jnp.argmax(lg, -1)
    gprob = jnp.max(jax.nn.softmax(lg, -1), -1, keepdims=True)
    le = ((x @ we).astype(jnp.float32) + be).reshape(n, MOE_GROUPS, MOE_EXPERTS_PER_GROUP)
    le = le[jnp.arange(n), gsel]
    top_v, top_i = lax.top_k(le, MOE_TOPK)
    gate = jax.nn.softmax(top_v, -1) * gprob
    eid = gsel[:, None].astype(jnp.int32) * MOE_EXPERTS_PER_GROUP + top_i.astype(jnp.int32)
    return _expert_dispatch(x, eid, gate, w1, w3, w2).reshape(shp)


def _ffn_sublayer(x, mods, g, b, wg, bg, we, be, w1, w3, w2):
    h = x * (1 + mods[4]) + mods[3]
    return _layernorm(DEEPNORM_ALPHA * x + mods[5] * _moe(h, wg, bg, we, be, w1, w3, w2), g, b)


def setup_inputs(seed: int = 0) -> dict:
    key = jax.random.key(seed)
    it = iter(jax.random.split(key, 48))

    def nrm(shape, scale=1.0):
        return jax.random.normal(next(it), shape, jnp.float32) * scale

    d = D_MODEL
    n_pages = PAST_LEN // PAGE_SIZE
    n_phys = (DEC_BATCH * n_pages * 5 + 3) // 4
    perm = jax.random.permutation(next(it), n_phys).astype(jnp.int32)
    page_table = perm[:DEC_BATCH * n_pages].reshape(DEC_BATCH, n_pages)
    dil_bufs = [nrm((N_DIL_LAYERS, DEC_BATCH, min(w, PAST_LEN), 2, DIL_HEADS, DIL_DH)) for (w, _) in DIL_PATTERNS]
    diff_width = DIFF_HEADS * 2 * DIFF_DH
    return {
        'x_prompt': nrm((BATCH, SEQ, d)),
        'x_sample': nrm((DEC_BATCH, DEC_SEQ, d)),
        'cache_diff_k': nrm((N_DIFF_LAYERS, n_phys, PAGE_SIZE, DIFF_KV_HEADS, 2 * DIFF_DH)),
        'cache_diff_v': nrm((N_DIFF_LAYERS, n_phys, PAGE_SIZE, DIFF_KV_HEADS, 2 * DIFF_DH)),
        'cache_dil_kv_g0': dil_bufs[0],
        'cache_dil_kv_g1': dil_bufs[1],
        'cache_dil_kv_g2': dil_bufs[2],
        'cache_mla_ckv': nrm((N_MLA_LAYERS, n_phys, PAGE_SIZE, MLA_KV_RANK)),
        'cache_mla_krope': nrm((N_MLA_LAYERS, n_phys, PAGE_SIZE, MLA_ROPE)),
        'page_table': page_table,
        'c_prompt': nrm((BATCH, d)),
        'c_sample': nrm((DEC_BATCH, d)),
        'rel_bias': nrm((N_BUCKETS, REL_HEADS), 0.5),
        'ada_w': nrm((DEPTH, d, 6 * d), d ** -0.5),
        'ada_b': nrm((DEPTH, 6 * d), 0.02),
        'ln_g': 1.0 + nrm((DEPTH, 2, d), 0.02),
        'ln_b': nrm((DEPTH, 2, d), 0.02),
        'diff_w_in': nrm((N_DIFF_LAYERS, d, diff_width + 2 * DIFF_KV_HEADS * 2 * DIFF_DH), d ** -0.5),
        'diff_lam': nrm((N_DIFF_LAYERS, 4, DIFF_DH), 0.1),
        'diff_subln': 1.0 + nrm((N_DIFF_LAYERS, 2 * DIFF_DH), 0.02),
        'diff_w_out': nrm((N_DIFF_LAYERS, diff_width, d), diff_width ** -0.5 * DEEPNORM_BETA),
        'dil_w_in': nrm((N_DIL_LAYERS, d, len(DIL_PATTERNS) * 3 * DIL_HEADS * DIL_DH), d ** -0.5),
        'dil_w_out': nrm((N_DIL_LAYERS, DIL_HEADS * DIL_DH, d), (DIL_HEADS * DIL_DH) ** -0.5 * DEEPNORM_BETA),
        'mla_w_in': nrm((N_MLA_LAYERS, d, MLA_Q_RANK + MLA_KV_RANK + MLA_ROPE), d ** -0.5),
        'mla_q_norm': 1.0 + nrm((N_MLA_LAYERS, MLA_Q_RANK), 0.02),
        'mla_kv_norm': 1.0 + nrm((N_MLA_LAYERS, MLA_KV_RANK), 0.02),
        'mla_w_uq': nrm((N_MLA_LAYERS, MLA_Q_RANK, MLA_HEADS * (MLA_NOPE + MLA_ROPE)), MLA_Q_RANK ** -0.5),
        'mla_w_uk': nrm((N_MLA_LAYERS, MLA_KV_RANK, MLA_HEADS, MLA_NOPE), MLA_KV_RANK ** -0.5),
        'mla_w_uv': nrm((N_MLA_LAYERS, MLA_KV_RANK, MLA_HEADS, MLA_V), MLA_KV_RANK ** -0.5),
        'mla_w_out': nrm((N_MLA_LAYERS, MLA_HEADS * MLA_V, d), (MLA_HEADS * MLA_V) ** -0.5 * DEEPNORM_BETA),
        'moe_wg': nrm((DEPTH, d, MOE_GROUPS), d ** -0.5),
        'moe_bg': nrm((DEPTH, MOE_GROUPS), 0.01),
        'moe_we': nrm((DEPTH, d, MOE_EXPERTS), d ** -0.5),
        'moe_be': nrm((DEPTH, MOE_EXPERTS), 0.01),
        'moe_w1': nrm((DEPTH, MOE_EXPERTS, d, MOE_FF), d ** -0.5),
        'moe_w3': nrm((DEPTH, MOE_EXPERTS, d, MOE_FF), d ** -0.5),
        'moe_w2': nrm((DEPTH, MOE_EXPERTS, MOE_FF, d), MOE_FF ** -0.5 * DEEPNORM_BETA),
    }


def reference(x_prompt, x_sample, cache_diff_k, cache_diff_v, cache_dil_kv_g0, cache_dil_kv_g1, cache_dil_kv_g2,
              cache_mla_ckv, cache_mla_krope, page_table, c_prompt, c_sample, rel_bias, ada_w, ada_b, ln_g, ln_b,
              diff_w_in, diff_lam, diff_subln, diff_w_out, dil_w_in, dil_w_out, mla_w_in, mla_q_norm, mla_kv_norm,
              mla_w_uq, mla_w_uk, mla_w_uv, mla_w_out, moe_wg, moe_bg, moe_we, moe_be, moe_w1, moe_w3, moe_w2):
    past = page_table.shape[1] * cache_diff_k.shape[2]
    dil_cache = (cache_dil_kv_g0, cache_dil_kv_g1, cache_dil_kv_g2)
    yp, ys = x_prompt, x_sample
    dk_p, dv_p, dk_s, dv_s = [], [], [], []
    dl_p = [[] for _ in DIL_PATTERNS]
    dl_s = [[] for _ in DIL_PATTERNS]
    ck_p, kr_p, ck_s, kr_s = [], [], [], []
    for i in range(DEPTH):
        kind, j = i % N_MIXERS, i // N_MIXERS
        mod_p = _ada(c_prompt, ada_w[i], ada_b[i])
        mod_s = _ada(c_sample, ada_w[i], ada_b[i])
        hp = yp * (1 + mod_p[1]) + mod_p[0]
        hs = ys * (1 + mod_s[1]) + mod_s[0]
        if kind == 0:
            o_p, kp, vp = _diff_prompt(hp, diff_w_in[j], diff_lam[j], diff_subln[j], diff_w_out[j], rel_bias, i)
            o_s, ks, vs = _diff_sample(hs, cache_diff_k, cache_diff_v, j, page_table, diff_w_in[j], diff_lam[j],
                                       diff_subln[j], diff_w_out[j], rel_bias, i)
            dk_p.append(kp)
            dv_p.append(vp)
            dk_s.append(ks)
            dv_s.append(vs)
        elif kind == 1:
            o_p, bp = _dil_prompt(hp, dil_w_in[j], dil_w_out[j], rel_bias)
            o_s, bs = _dil_sample(hs, [cb[j] for cb in dil_cache], past, dil_w_in[j], dil_w_out[j], rel_bias)
            for g in range(len(DIL_PATTERNS)):
                dl_p[g].append(bp[g])
                dl_s[g].append(bs[g])
        else:
            o_p, cp, rp = _mla_prompt(hp, mla_w_in[j], mla_q_norm[j], mla_kv_norm[j], mla_w_uq[j], mla_w_uk[j],
                                      mla_w_uv[j], mla_w_out[j])
            o_s, cs, rs = _mla_sample(hs, cache_mla_ckv, cache_mla_krope, j, page_table, mla_w_in[j], mla_q_norm[j],
                                      mla_kv_norm[j], mla_w_uq[j], mla_w_uk[j], mla_w_uv[j], mla_w_out[j])
            ck_p.append(cp)
            kr_p.append(rp)
            ck_s.append(cs)
            kr_s.append(rs)
        yp = _layernorm(DEEPNORM_ALPHA * yp + mod_p[2] * o_p, ln_g[i, 0], ln_b[i, 0])
        ys = _layernorm(DEEPNORM_ALPHA * ys + mod_s[2] * o_s, ln_g[i, 0], ln_b[i, 0])
        yp = _ffn_sublayer(yp, mod_p, ln_g[i, 1], ln_b[i, 1], moe_wg[i], moe_bg[i], moe_we[i], moe_be[i],
                           moe_w1[i], moe_w3[i], moe_w2[i])
        ys = _ffn_sublayer(ys, mod_s, ln_g[i, 1], ln_b[i, 1], moe_wg[i], moe_bg[i], moe_we[i], moe_be[i],
                           moe_w1[i], moe_w3[i], moe_w2[i])
    diff_k_prompt = jnp.stack(dk_p)
    diff_v_prompt = jnp.stack(dv_p)
    diff_k_sample = jnp.stack(dk_s)
    diff_v_sample = jnp.stack(dv_s)
    dil_prompt = [jnp.stack(v) for v in dl_p]
    dil_sample = [jnp.stack(v) for v in dl_s]
    mla_ckv_prompt = jnp.stack(ck_p)
    mla_krope_prompt = jnp.stack(kr_p)
    mla_ckv_sample = jnp.stack(ck_s)
    mla_krope_sample = jnp.stack(kr_s)
    return (yp, ys, diff_k_prompt, diff_v_prompt, diff_k_sample, diff_v_sample,
            dil_prompt[0], dil_prompt[1], dil_prompt[2], dil_sample[0], dil_sample[1], dil_sample[2],
            mla_ckv_prompt, mla_krope_prompt, mla_ckv_sample, mla_krope_sample)
```

```python
import functools
import math

import jax
import jax.numpy as jnp
from jax import lax
from jax.experimental import pallas as pl
from jax.experimental.pallas import tpu as pltpu

F32 = jnp.float32
BF16 = jnp.bfloat16

N_MIXERS = 3
N_BUCKETS = 32
REL_MAX_DIST = 2048
DIFF_DH = 64
DIFF_HEADS = 8
DIFF_KV_HEADS = 2
DIL_PATTERNS = ((128, 1), (512, 4), (2048, 16))
DIL_HEADS = 8
DIL_DH = 64
MLA_HEADS = 16
MLA_NOPE = 64
MLA_ROPE = 32
MLA_V = 64
MLA_Q_RANK = 384
MLA_KV_RANK = 256
ROPE_BASE = 10000.0
MOE_GROUPS = 4
MOE_EXPERTS_PER_GROUP = 8
MOE_EXPERTS = MOE_GROUPS * MOE_EXPERTS_PER_GROUP
MOE_TOPK = 2
LN_EPS = 1e-5
NEG_INF = -1e30

LANES = 128
ROW_TILE = 512
MOE_ROWS = 256
DIFF_TILE = 256
MLA_TILE = 512
DIL_TILE = 128
PAGES_PER_STEP = 16
VMEM_LIMIT_BYTES = 56 * 1024 * 1024


def _params(*sem):
    return pltpu.CompilerParams(dimension_semantics=sem, vmem_limit_bytes=VMEM_LIMIT_BYTES)


def _dot(a, b):
    return jnp.dot(a, b, preferred_element_type=F32)


def _dot_nt(a, b):
    return lax.dot_general(a, b, (((1,), (1,)), ((), ())), preferred_element_type=F32)


def _resident(shape):
    nd = len(shape)
    return pl.BlockSpec(shape, lambda *_: (0,) * nd)


def _lane(shape, axis):
    return lax.broadcasted_iota(jnp.int32, shape, axis)


def _rel_bucket(dist):
    max_exact = N_BUCKETS // 2
    n = jnp.maximum(dist, 0)
    nf = jnp.maximum(n, 1).astype(F32)
    large = max_exact + (jnp.log(nf / max_exact) / math.log(REL_MAX_DIST / max_exact)
                         * (N_BUCKETS - max_exact)).astype(jnp.int32)
    return jnp.where(n < max_exact, n, jnp.minimum(large, N_BUCKETS - 1))


def _bias_by_dist(rel_bias, n):
    return rel_bias[_rel_bucket(jnp.arange(n))].astype(F32).T


def _rope_tables(pos):
    half = MLA_ROPE // 2
    inv = ROPE_BASE ** (-jnp.arange(half, dtype=F32) / half)
    ang = pos.astype(F32)[:, None] * inv[None, :]
    cos, sin = jnp.cos(ang), jnp.sin(ang)
    n = pos.shape[0]
    cq = jnp.concatenate([jnp.ones((n, MLA_NOPE), F32), cos, cos, jnp.zeros((n, 32), F32)], 1)
    sq = jnp.concatenate([jnp.zeros((n, MLA_NOPE), F32), -sin, sin, jnp.zeros((n, 32), F32)], 1)
    ck = jnp.concatenate([cos, cos, jnp.zeros((n, LANES - MLA_ROPE), F32)], 1)
    sk = jnp.concatenate([-sin, sin, jnp.zeros((n, LANES - MLA_ROPE), F32)], 1)
    return cq[None], sq[None], ck[None], sk[None]


def _ada_kernel(c_ref, w_ref, b_ref, o_ref):
    c = c_ref[...]
    a = (c * jax.nn.sigmoid(c)).astype(BF16)
    o_ref[...] = _dot(a, w_ref[...].astype(BF16)) + b_ref[...]


def _ada(c_all, ada_w, ada_b):
    depth, d, n = ada_w.shape
    m = c_all.shape[0]
    tn = 1536
    return pl.pallas_call(
        _ada_kernel,
        grid=(depth, n // tn),
        in_specs=[pl.BlockSpec((m, d), lambda l, j: (0, 0)),
                  pl.BlockSpec((None, d, tn), lambda l, j: (l, 0, j)),
                  pl.BlockSpec((None, 1, tn), lambda l, j: (l, 0, j))],
        out_specs=pl.BlockSpec((None, m, tn), lambda l, j: (l, 0, j)),
        out_shape=jax.ShapeDtypeStruct((depth, m, n), F32),
        compiler_params=_params("arbitrary", "arbitrary"),
        name="ada",
    )(c_all, ada_w, ada_b.reshape(depth, 1, n))


def _row_blocks(b, s):
    if s >= ROW_TILE:
        assert s % ROW_TILE == 0
        return 1, ROW_TILE
    tb = min(b, max(1, ROW_TILE // s))
    assert b % tb == 0
    return tb, s


def _x_spec(tb, ts, s, d):
    nst = s // ts
    return pl.BlockSpec((tb, ts, d), lambda i: (i // nst, i % nst, 0))


def _mod_spec(tb, ts, s, d):
    nst = s // ts
    return pl.BlockSpec((tb, 1, d), lambda i: (i // nst, 0, 0))


def _pos_spec(ts, s, c):
    nst = s // ts
    return pl.BlockSpec((1, ts, c), lambda i: (0, i % nst, 0))


def _rows_spec(rows, n, off=0):
    return pl.BlockSpec((rows, n), lambda i: (i + off, 0))


def _proj_kernel(x_ref, sh_ref, sc_ref, w_ref, *o_refs, splits):
    tb, ts, d = x_ref.shape
    h = x_ref[...] * (1.0 + sc_ref[...]) + sh_ref[...]
    h = h.reshape(tb * ts, d).astype(BF16)
    for o_ref, (c0, c1) in zip(o_refs, splits):
        for s in range(c0, c1, 512):
            e = min(s + 512, c1)
            o_ref[:, s - c0:e - c0] = _dot(h, w_ref[:, s:e]).astype(o_ref.dtype)


def _proj(x, shift, scale, w, splits, dtypes):
    b, s, d = x.shape
    tb, ts = _row_blocks(b, s)
    rows = tb * ts
    n_tiles = b * s // rows
    return pl.pallas_call(
        functools.partial(_proj_kernel, splits=tuple(splits)),
        grid=(n_tiles,),
        in_specs=[_x_spec(tb, ts, s, d), _mod_spec(tb, ts, s, d), _mod_spec(tb, ts, s, d),
                  _resident(w.shape)],
        out_specs=[_rows_spec(rows, c1 - c0) for (c0, c1) in splits],
        out_shape=[jax.ShapeDtypeStruct((b * s, c1 - c0), dt) for (c0, c1), dt in zip(splits, dtypes)],
        compiler_params=_params("arbitrary"),
        name="proj",
    )(x, shift, scale, w)


def _layernorm(z, g, b):
    mu = jnp.mean(z, -1, keepdims=True)
    zc = z - mu
    var = jnp.mean(zc * zc, -1, keepdims=True)
    return zc * lax.rsqrt(var + LN_EPS) * g + b


def _outproj_ln_kernel(o_ref, x_ref, gate_ref, w_ref, g_ref, b_ref, y_ref, *, alpha):
    tb, ts, d = x_ref.shape
    f = _dot(o_ref[...].astype(BF16), w_ref[...]).reshape(tb, ts, d)
    z = alpha * x_ref[...] + gate_ref[...] * f
    y_ref[...] = _layernorm(z, g_ref[...], b_ref[...])


def _outproj_ln(o2d, x, gate, w, g, b, alpha):
    bb, s, d = x.shape
    tb, ts = _row_blocks(bb, s)
    rows = tb * ts
    return pl.pallas_call(
        functools.partial(_outproj_ln_kernel, alpha=alpha),
        grid=(bb * s // rows,),
        in_specs=[_rows_spec(rows, o2d.shape[1]), _x_spec(tb, ts, s, d), _mod_spec(tb, ts, s, d),
                  _resident(w.shape), _resident((1, 1, d)), _resident((1, 1, d))],
        out_specs=_x_spec(tb, ts, s, d),
        out_shape=jax.ShapeDtypeStruct(x.shape, F32),
        compiler_params=_params("arbitrary"),
        name="outproj_ln",
    )(o2d, x, gate, w, g.reshape(1, 1, d), b.reshape(1, 1, d))


def _merge_outproj_ln_kernel(o0, o1, o2, l0, l1, l2, x_ref, gate_ref, w_ref, g_ref, b_ref, y_ref, *, alpha):
    tb, ts, d = x_ref.shape
    a0, a1, a2 = l0[...], l1[...], l2[...]
    mx = jnp.maximum(jnp.maximum(a0, a1), a2)
    e0, e1, e2 = jnp.exp(a0 - mx), jnp.exp(a1 - mx), jnp.exp(a2 - mx)
    o = (e0 * o0[...] + e1 * o1[...] + e2 * o2[...]) / (e0 + e1 + e2)
    f = _dot(o.astype(BF16), w_ref[...]).reshape(tb, ts, d)
    z = alpha * x_ref[...] + gate_ref[...] * f
    y_ref[...] = _layernorm(z, g_ref[...], b_ref[...])


def _merge_outproj_ln(outs, lses, x, gate, w, g, b, alpha):
    bb, s, d = x.shape
    tb, ts = _row_blocks(bb, s)
    rows = tb * ts
    k = outs[0].shape[1]
    return pl.pallas_call(
        functools.partial(_merge_outproj_ln_kernel, alpha=alpha),
        grid=(bb * s // rows,),
        in_specs=[_rows_spec(rows, k)] * 6 + [_x_spec(tb, ts, s, d), _mod_spec(tb, ts, s, d),
                                              _resident(w.shape), _resident((1, 1, d)), _resident((1, 1, d))],
        out_specs=_x_spec(tb, ts, s, d),
        out_shape=jax.ShapeDtypeStruct(x.shape, F32),
        compiler_params=_params("arbitrary"),
        name="merge_outproj_ln",
    )(*outs, *lses, x, gate, w, g.reshape(1, 1, d), b.reshape(1, 1, d))


def _diff_prompt_kernel(lam_ref, q_ref, k_ref, v_ref, bias_ref, subln_ref, o_ref, m_sc, l_sc, acc_sc,
                        *, scale, post):
    qi, ki = pl.program_id(2), pl.program_id(3)
    r_heads = q_ref.shape[1] // LANES

    @pl.when(ki == 0)
    def _():
        m_sc[...] = jnp.full(m_sc.shape, NEG_INF, F32)
        l_sc[...] = jnp.zeros(l_sc.shape, F32)
        acc_sc[...] = jnp.zeros(acc_sc.shape, F32)

    @pl.when(ki <= qi)
    def _():
        k = k_ref[...].astype(BF16)
        v = v_ref[...].astype(BF16)
        first = _lane(k.shape, 1) < DIFF_DH
        kmaps = (jnp.where(first, k, jnp.zeros_like(k)), jnp.where(first, jnp.zeros_like(k), k))
        for r in range(r_heads):
            q = q_ref[:, r * LANES:(r + 1) * LANES]
            bias = bias_ref[r, qi - ki]
            for mi in range(2):
                i = 2 * r + mi
                s = _dot_nt(q, kmaps[mi]) * scale + bias
                m_prev = m_sc[i]
                m_new = jnp.maximum(m_prev, jnp.max(s, -1, keepdims=True))
                a = jnp.exp(m_prev - m_new)
                p = jnp.exp(s - m_new)
                l_sc[i] = a * l_sc[i] + jnp.sum(p, -1, keepdims=True)
                acc_sc[i] = a * acc_sc[i] + _dot(p.astype(BF16), v)
                m_sc[i] = m_new

    @pl.when(ki == pl.num_programs(3) - 1)
    def _():
        lam = lam_ref[0, 0]
        for r in range(r_heads):
            o = acc_sc[2 * r] / l_sc[2 * r] - lam * (acc_sc[2 * r + 1] / l_sc[2 * r + 1])
            o = o * lax.rsqrt(jnp.mean(o * o, -1, keepdims=True) + LN_EPS) * subln_ref[...] * post
            o_ref[:, r * LANES:(r + 1) * LANES] = o.astype(o_ref.dtype)


def _diff_prompt_attn(q2d, k2d, v2d, bias_d, lam, lam_init, subln, b, s):
    t = min(DIFF_TILE, s)
    nt = s // t
    r = DIFF_HEADS // DIFF_KV_HEADS
    i = jnp.arange(t)
    dist = (jnp.arange(nt) * t)[:, None, None] + i[None, :, None] - i[None, None, :]
    tiles = jnp.where(dist >= 0, bias_d[:, jnp.maximum(dist, 0)], NEG_INF)
    hw = r * LANES
    return pl.pallas_call(
        functools.partial(_diff_prompt_kernel, scale=DIFF_DH ** -0.5, post=1.0 - lam_init),
        grid=(DIFF_KV_HEADS, b, nt, nt),
        in_specs=[pl.BlockSpec(memory_space=pltpu.SMEM),
                  pl.BlockSpec((None, t, hw), lambda g, bb, qi, ki: (bb, qi, g)),
                  pl.BlockSpec((None, t, LANES), lambda g, bb, qi, ki: (bb, jnp.minimum(ki, qi), g)),
                  pl.BlockSpec((None, t, LANES), lambda g, bb, qi, ki: (bb, jnp.minimum(ki, qi), g)),
                  pl.BlockSpec((r, nt, t, t), lambda g, bb, qi, ki: (g, 0, 0, 0)),
                  pl.BlockSpec((1, LANES), lambda g, bb, qi, ki: (0, 0))],
        out_specs=pl.BlockSpec((None, t, hw), lambda g, bb, qi, ki: (bb, qi, g)),
        out_shape=jax.ShapeDtypeStruct((b, s, DIFF_HEADS * LANES), BF16),
        scratch_shapes=[pltpu.VMEM((2 * r, t, 1), F32), pltpu.VMEM((2 * r, t, 1), F32),
                        pltpu.VMEM((2 * r, t, LANES), F32)],
        compiler_params=_params("arbitrary", "arbitrary", "arbitrary", "arbitrary"),
        name="diff_prompt_attn",
    )(lam.reshape(1, 1), q2d.reshape(b, s, -1), k2d.reshape(b, s, -1), v2d.reshape(b, s, -1), tiles,
      subln.reshape(1, LANES)).reshape(b * s, -1)


def _diff_sample_kernel(pt_ref, lam_ref, q_ref, kn_ref, vn_ref, bias_ref, subln_ref, *rest, npg, scale, post):
    k_refs, v_refs = rest[:npg], rest[npg:2 * npg]
    o_ref, qbd, m_sc, l_sc, acc_sc = rest[2 * npg:]
    c = pl.program_id(1)
    nc = pl.num_programs(1)
    t = q_ref.shape[0]
    rows = 2 * DIFF_HEADS * t

    @pl.when(c == 0)
    def _():
        q = q_ref[...]
        first = _lane((t, LANES), 1) < DIFF_DH
        zero = jnp.zeros((t, LANES), F32)
        heads = [q[:, h * LANES:(h + 1) * LANES] for h in range(DIFF_HEADS)]
        pieces = [jnp.where(first, qh, zero) for qh in heads] + [jnp.where(first, zero, qh) for qh in heads]
        qbd[...] = jnp.concatenate(pieces, 0).astype(BF16)
        m_sc[...] = jnp.full(m_sc.shape, NEG_INF, F32)
        l_sc[...] = jnp.zeros(l_sc.shape, F32)
        acc_sc[...] = jnp.zeros(acc_sc.shape, F32)

    def step(kp, vp, bias):
        s = _dot_nt(qbd[...], kp.astype(BF16)) * scale + bias
        m_prev = m_sc[...]
        m_new = jnp.maximum(m_prev, jnp.max(s, -1, keepdims=True))
        a = jnp.exp(m_prev - m_new)
        p = jnp.exp(s - m_new)
        l_sc[...] = a * l_sc[...] + jnp.sum(p, -1, keepdims=True)
        acc_sc[...] = a * acc_sc[...] + _dot(p.astype(BF16), vp.astype(BF16))
        m_sc[...] = m_new

    for i in range(npg):
        step(k_refs[i][...], v_refs[i][...], bias_ref[c * npg + i])

    @pl.when(c == nc - 1)
    def _():
        pad = jnp.zeros((k_refs[0].shape[0] - kn_ref.shape[0], LANES), F32)
        step(jnp.concatenate([kn_ref[...], pad], 0), jnp.concatenate([vn_ref[...], pad], 0),
             bias_ref[nc * npg])
        half = rows // 2
        o = acc_sc[...] / l_sc[...]
        o = o[:half] - lam_ref[0, 0] * o[half:]
        o = o * lax.rsqrt(jnp.mean(o * o, -1, keepdims=True) + LN_EPS) * subln_ref[...] * post
        o_ref[...] = o


def _diff_sample_attn(q2d, k2d, v2d, cache_k, cache_v, slot, page_table, bias_d, lam, lam_init, subln, db, t):
    n_phys, page = cache_k.shape[1], cache_k.shape[2]
    n_pages = page_table.shape[1]
    past = n_pages * page
    npg = min(PAGES_PER_STEP, n_pages)
    assert n_pages % npg == 0
    prow = page * DIFF_KV_HEADS
    ck = cache_k.reshape(cache_k.shape[0], n_phys, prow, LANES)
    cv = cache_v.reshape(cache_v.shape[0], n_phys, prow, LANES)
    rows = 2 * DIFF_HEADS * t
    ridx = jnp.arange(rows)
    rh, rt = (ridx % (DIFF_HEADS * t)) // t, ridx % t
    cidx = jnp.arange(prow)
    cpos, ckv = cidx // DIFF_KV_HEADS, cidx % DIFF_KV_HEADS
    pos = (jnp.arange(n_pages + 1) * page)[:, None, None] + cpos[None, None, :]
    dist = past + rt[None, :, None] - pos
    ok = (ckv[None, None, :] == (rh // (DIFF_HEADS // DIFF_KV_HEADS))[None, :, None]) & (dist >= 0)
    table = jnp.where(ok, bias_d[rh[None, :, None], jnp.maximum(dist, 0)], NEG_INF)

    def page_spec(i):
        return pl.BlockSpec((None, None, prow, LANES), lambda b, c, pt: (slot, pt[b, c * npg + i], 0, 0))

    grid_spec = pltpu.PrefetchScalarGridSpec(
        num_scalar_prefetch=1,
        grid=(db, n_pages // npg),
        in_specs=[pl.BlockSpec(memory_space=pltpu.SMEM),
                  pl.BlockSpec((None, t, DIFF_HEADS * LANES), lambda b, c, pt: (b, 0, 0)),
                  pl.BlockSpec((None, t * DIFF_KV_HEADS, LANES), lambda b, c, pt: (b, 0, 0)),
                  pl.BlockSpec((None, t * DIFF_KV_HEADS, LANES), lambda b, c, pt: (b, 0, 0)),
                  pl.BlockSpec(table.shape, lambda b, c, pt: (0, 0, 0)),
                  pl.BlockSpec((1, LANES), lambda b, c, pt: (0, 0))]
        + [page_spec(i) for i in range(npg)] + [page_spec(i) for i in range(npg)],
        out_specs=pl.BlockSpec((None, rows // 2, LANES), lambda b, c, pt: (b, 0, 0)),
        scratch_shapes=[pltpu.VMEM((rows, LANES), BF16), pltpu.VMEM((rows, 1), F32), pltpu.VMEM((rows, 1), F32),
                        pltpu.VMEM((rows, LANES), F32)])
    o = pl.pallas_call(
        functools.partial(_diff_sample_kernel, npg=npg, scale=DIFF_DH ** -0.5, post=1.0 - lam_init),
        grid_spec=grid_spec,
        out_shape=jax.ShapeDtypeStruct((db, rows // 2, LANES), F32),
        compiler_params=_params("arbitrary", "arbitrary"),
        name="diff_sample_attn",
    )(page_table, lam.reshape(1, 1), q2d.reshape(db, t, -1), k2d.reshape(db, t * DIFF_KV_HEADS, LANES),
      v2d.reshape(db, t * DIFF_KV_HEADS, LANES), table, subln.reshape(1, LANES),
      *([ck] * npg), *([cv] * npg))
    return o.reshape(db, DIFF_HEADS, t, LANES).transpose(0, 2, 1, 3).reshape(db * t, DIFF_HEADS * LANES)


def _dil_prompt_kernel(q_ref, kp_ref, kc_ref, vp_ref, vc_ref, bias_ref, o_ref, lse_ref, *, scale):
    qi = pl.program_id(2)
    tq = q_ref.shape[0]
    q = q_ref[...]
    k = jnp.concatenate([kp_ref[...], kc_ref[...]], 0).astype(BF16)
    v = jnp.concatenate([vp_ref[...], vc_ref[...]], 0).astype(BF16)
    col = _lane((tq, 2 * tq), 1)
    prev_ok = (col >= tq) | (qi > 0)
    first = _lane((tq, LANES), 1) < DIL_DH
    for p in range(DIL_HEADS // 2):
        sl = slice(p * LANES, (p + 1) * LANES)
        qp, kp, vp = q[:, sl], k[:, sl], v[:, sl]
        outs, lses = [], []
        for a in range(2):
            qa = jnp.where(first, qp, jnp.zeros_like(qp)) if a == 0 else jnp.where(first, jnp.zeros_like(qp), qp)
            s = _dot_nt(qa, kp) * scale + bias_ref[2 * p + a]
            s = jnp.where(prev_ok, s, NEG_INF)
            m = jnp.max(s, -1, keepdims=True)
            e = jnp.exp(s - m)
            l = jnp.sum(e, -1, keepdims=True)
            outs.append(_dot(e.astype(BF16), vp) / l)
            lses.append(m + jnp.log(l))
        o_ref[:, sl] = jnp.where(first, outs[0], outs[1])
        lse_ref[:, sl] = jnp.where(first, jnp.broadcast_to(lses[0], (tq, LANES)),
                                   jnp.broadcast_to(lses[1], (tq, LANES)))


def _dil_prompt_group(q2d, kv2d, bias_d, window, dil, b, s):
    ln = s // dil
    band = window // dil
    tq = min(DIL_TILE, ln)
    assert band <= tq or ln == tq
    nb = ln // tq
    hw = DIL_HEADS * DIL_DH
    i = jnp.arange(tq)[:, None]
    m = jnp.arange(2 * tq)[None, :]
    delta = i + tq - m
    ok = (delta >= 0) & (delta <= band)
    table = jnp.where(ok[None], bias_d[:, jnp.clip(delta, 0, band) * dil], NEG_INF)
    qv = q2d.reshape(b, ln, dil * hw)
    kvv = kv2d.reshape(b, ln, dil * 2 * hw)

    def kv_spec(prev, val):
        return pl.BlockSpec((None, tq, hw),
                            lambda bb, c, qi: (bb, jnp.maximum(qi - 1, 0) if prev else qi, 2 * c + val))

    o, lse = pl.pallas_call(
        functools.partial(_dil_prompt_kernel, scale=DIL_DH ** -0.5),
        grid=(b, dil, nb),
        in_specs=[pl.BlockSpec((None, tq, hw), lambda bb, c, qi: (bb, qi, c)),
                  kv_spec(True, 0), kv_spec(False, 0), kv_spec(True, 1), kv_spec(False, 1),
                  pl.BlockSpec(table.shape, lambda bb, c, qi: (0, 0, 0))],
        out_specs=[pl.BlockSpec((None, tq, hw), lambda bb, c, qi: (bb, qi, c))] * 2,
        out_shape=[jax.ShapeDtypeStruct((b, ln, dil * hw), F32)] * 2,
        compiler_params=_params("arbitrary", "arbitrary", "arbitrary"),
        name="dil_prompt_attn",
    )(qv, kvv, kvv, kvv, kvv, table)
    return o.reshape(b * s, hw), lse.reshape(b * s, hw)


def _dil_sample_kernel(q0, q1, q2, kv0, kv1, kv2, b0, b1, b2, o_ref, *, scale):
    t = q0.shape[0]
    hw = DIL_HEADS * DIL_DH
    rows = DIL_HEADS * t
    diag = (_lane((rows, hw), 1) // DIL_DH) == (_lane((rows, hw), 0) // t)
    outs, lses = [], []
    for q_ref, kv_ref, b_ref in ((q0, kv0, b0), (q1, kv1, b1), (q2, kv2, b2)):
        q = q_ref[...]
        qbd = jnp.where(diag, jnp.concatenate([q] * DIL_HEADS, 0), 0.0).astype(BF16)
        s = _dot_nt(qbd, kv_ref[:, :hw]) * scale + b_ref[...]
        m = jnp.max(s, -1, keepdims=True)
        e = jnp.exp(s - m)
        l = jnp.sum(e, -1, keepdims=True)
        outs.append(_dot(e.astype(BF16), kv_ref[:, hw:]) / l)
        lses.append(m + jnp.log(l))
    mx = jnp.maximum(jnp.maximum(lses[0], lses[1]), lses[2])
    es = [jnp.exp(x - mx) for x in lses]
    o = (es[0] * outs[0] + es[1] * outs[1] + es[2] * outs[2]) / (es[0] + es[1] + es[2])
    o = jnp.where(diag, o, 0.0)
    o_ref[...] = jnp.sum(o.reshape(DIL_HEADS, t, hw), 0)


def _dil_sample_attn(qs, kv_news, bufs, bias_d, past, db, t):
    hw = DIL_HEADS * DIL_DH
    rows = DIL_HEADS * t
    kvs, tables = [], []
    for (window, dil), kvn, buf in zip(DIL_PATTERNS, kv_news, bufs):
        wb = buf.shape[1]
        pad = -(wb + t) % LANES
        kvs.append(jnp.concatenate([buf.reshape(db, wb, 2 * hw).astype(BF16),
                                    kvn.reshape(db, t, 2 * hw).astype(BF16),
                                    jnp.zeros((db, pad, 2 * hw), BF16)], 1))
        ridx = jnp.arange(rows)
        rh, rt = ridx // t, ridx % t
        delta = wb + rt[:, None] - jnp.arange(wb + t + pad)[None, :]
        ok = (delta >= 0) & (delta % dil == 0) & (delta <= window) & (past - delta + rt[:, None] >= 0)
        tables.append(jnp.where(ok, bias_d[rh[:, None], jnp.clip(delta, 0, window)], NEG_INF))
    o = pl.pallas_call(
        functools.partial(_dil_sample_kernel, scale=DIL_DH ** -0.5),
        grid=(db,),
        in_specs=[pl.BlockSpec((None, t, hw), lambda b: (b, 0, 0))] * 3
        + [pl.BlockSpec((None,) + kv.shape[1:], lambda b: (b, 0, 0)) for kv in kvs]
        + [_resident(tb.shape) for tb in tables],
        out_specs=pl.BlockSpec((None, t, hw), lambda b: (b, 0, 0)),
        out_shape=jax.ShapeDtypeStruct((db, t, hw), F32),
        compiler_params=_params("arbitrary"),
        name="dil_sample_attn",
    )(*[q.reshape(db, t, hw) for q in qs], *kvs, *tables)
    return o.reshape(db * t, hw)


def _mla_proj_kernel(x_ref, sh_ref, sc_ref, cq_ref, sq_ref, ck_ref, sk_ref, win_ref, qn_ref, kvn_ref,
                     wqa_ref, wqb_ref, *rest, with_kv):
    if with_kv:
        wk_ref, wv_ref, ckv_ref, kr_ref, q_ref, k_ref, v_ref = rest
    else:
        ckv_ref, kr_ref, q_ref = rest
    tb, ts, d = x_ref.shape
    rows = tb * ts
    h = x_ref[...] * (1.0 + sc_ref[...]) + sh_ref[...]
    h = h.reshape(rows, d).astype(BF16)
    proj = _dot(h, win_ref[...])
    r0, r1 = MLA_Q_RANK, MLA_Q_RANK + MLA_KV_RANK

    def rms(z, g):
        return z * lax.rsqrt(jnp.mean(z * z, -1, keepdims=True) + LN_EPS) * g

    cq = rms(proj[:, :r0], qn_ref[...]).astype(BF16)
    ckv = rms(proj[:, r0:r1], kvn_ref[...])
    kra = proj[:, r1:r1 + LANES].reshape(tb, ts, LANES)
    krb = proj[:, r1 + LANES:r1 + 2 * LANES].reshape(tb, ts, LANES)
    kr = (kra * ck_ref[...] + krb * sk_ref[...]).reshape(rows, LANES)
    ckv_ref[...] = ckv
    kr_ref[...] = kr[:, :MLA_ROPE]
    nq = q_ref.shape[1]
    cqt = jnp.concatenate([cq_ref[...]] * MLA_HEADS, -1)
    sqt = jnp.concatenate([sq_ref[...]] * MLA_HEADS, -1)
    qa = _dot(cq, wqa_ref[...]).reshape(tb, ts, nq)
    qb = _dot(cq, wqb_ref[...]).reshape(tb, ts, nq)
    q_ref[...] = (qa * cqt + qb * sqt).reshape(rows, nq).astype(q_ref.dtype)
    if with_kv:
        kcat = jnp.concatenate([ckv.astype(BF16), kr.astype(BF16)], 1)
        k_ref[...] = _dot(kcat, wk_ref[...]).astype(k_ref.dtype)
        v_ref[...] = _dot(ckv.astype(BF16), wv_ref[...]).astype(v_ref.dtype)


def _mla_weights(w_in, w_uq, w_uk, w_uv):
    d = w_in.shape[0]
    r0, r1 = MLA_Q_RANK, MLA_Q_RANK + MLA_KV_RANK
    half = MLA_ROPE // 2
    zpad = jnp.zeros((d, LANES - MLA_ROPE), F32)
    kr_w = w_in[:, r1:]
    kr_sw = jnp.concatenate([kr_w[:, half:], kr_w[:, :half]], 1)
    win = jnp.concatenate([w_in[:, :r1], kr_w, zpad, kr_sw, zpad], 1).astype(BF16)
    hd = MLA_NOPE + MLA_ROPE
    wq = w_uq.reshape(MLA_Q_RANK, MLA_HEADS, hd)
    z32 = jnp.zeros((MLA_Q_RANK, MLA_HEADS, LANES - hd), F32)
    z64 = jnp.zeros((MLA_Q_RANK, MLA_HEADS, MLA_NOPE), F32)
    wqa = jnp.concatenate([wq, z32], -1).reshape(MLA_Q_RANK, MLA_HEADS * LANES).astype(BF16)
    wqb = jnp.concatenate([z64, wq[..., MLA_NOPE + half:], wq[..., MLA_NOPE:MLA_NOPE + half], z32], -1)
    wqb = wqb.reshape(MLA_Q_RANK, MLA_HEADS * LANES).astype(BF16)
    top = jnp.concatenate([w_uk, jnp.zeros((MLA_KV_RANK, MLA_HEADS, LANES - MLA_NOPE), F32)], -1)
    eye = jnp.eye(MLA_ROPE, dtype=F32)[:, None, :]
    mid = jnp.concatenate([jnp.zeros((MLA_ROPE, MLA_HEADS, MLA_NOPE), F32),
                           jnp.broadcast_to(eye, (MLA_ROPE, MLA_HEADS, MLA_ROPE)),
                           jnp.zeros((MLA_ROPE, MLA_HEADS, LANES - hd), F32)], -1)
    bot = jnp.zeros((LANES - MLA_ROPE, MLA_HEADS, LANES), F32)
    wk = jnp.concatenate([top, mid, bot], 0).reshape(MLA_KV_RANK + LANES, MLA_HEADS * LANES).astype(BF16)
    wv = w_uv.reshape(MLA_KV_RANK, MLA_HEADS * MLA_V).astype(BF16)
    a_top = jnp.concatenate([w_uk.transpose(1, 2, 0),
                             jnp.zeros((MLA_HEADS, MLA_NOPE, LANES), F32)], -1)
    a_mid = jnp.concatenate([jnp.zeros((MLA_ROPE, MLA_KV_RANK), F32), jnp.eye(MLA_ROPE, dtype=F32),
                             jnp.zeros((MLA_ROPE, LANES - MLA_ROPE), F32)], -1)
    a_mid = jnp.broadcast_to(a_mid[None], (MLA_HEADS, MLA_ROPE, MLA_KV_RANK + LANES))
    a_bot = jnp.zeros((MLA_HEADS, LANES - hd, MLA_KV_RANK + LANES), F32)
    wabs = jnp.concatenate([a_top, a_mid, a_bot], 1).reshape(MLA_HEADS * LANES, MLA_KV_RANK + LANES).astype(BF16)
    return win, wqa, wqb, wk, wv, wabs


def _mla_proj(x, shift, scale, pos, win, qn, kvn, wqa, wqb, wk, wv, with_kv):
    b, s, d = x.shape
    tb, ts = _row_blocks(b, s)
    rows = tb * ts
    cq, sq, ck, sk = _rope_tables(pos)
    nq = MLA_HEADS * LANES
    in_specs = [_x_spec(tb, ts, s, d), _mod_spec(tb, ts, s, d), _mod_spec(tb, ts, s, d),
                _pos_spec(ts, s, LANES), _pos_spec(ts, s, LANES), _pos_spec(ts, s, LANES), _pos_spec(ts, s, LANES),
                _resident(win.shape), _resident((1, MLA_Q_RANK)), _resident((1, MLA_KV_RANK)),
                _resident(wqa.shape), _resident(wqb.shape)]
    args = [x, shift, scale, cq, sq, ck, sk, win, qn.reshape(1, -1), kvn.reshape(1, -1), wqa, wqb]
    out_specs = [_rows_spec(rows, MLA_KV_RANK), _rows_spec(rows, MLA_ROPE), _rows_spec(rows, nq)]
    out_shape = [jax.ShapeDtypeStruct((b * s, MLA_KV_RANK), F32), jax.ShapeDtypeStruct((b * s, MLA_ROPE), F32),
                 jax.ShapeDtypeStruct((b * s, nq), BF16 if with_kv else F32)]
    if with_kv:
        in_specs += [_resident(wk.shape), _resident(wv.shape)]
        args += [wk, wv]
        out_specs += [_rows_spec(rows, nq), _rows_spec(rows, MLA_HEADS * MLA_V)]
        out_shape += [jax.ShapeDtypeStruct((b * s, nq), BF16), jax.ShapeDtypeStruct((b * s, MLA_HEADS * MLA_V), BF16)]
    return pl.pallas_call(
        functools.partial(_mla_proj_kernel, with_kv=with_kv),
        grid=(b * s // rows,),
        in_specs=in_specs, out_specs=out_specs, out_shape=out_shape,
        compiler_params=_params("arbitrary"),
        name="mla_proj",
    )(*args)


def _mla_prompt_kernel(q_ref, k_ref, v_ref, o_ref, m_sc, l_sc, acc_sc, *, scale):
    qi, ki = pl.program_id(2), pl.program_id(3)
    t = q_ref.shape[0]

    @pl.when(ki == 0)
    def _():
        m_sc[...] = jnp.full(m_sc.shape, NEG_INF, F32)
        l_sc[...] = jnp.zeros(l_sc.shape, F32)
        acc_sc[...] = jnp.zeros(acc_sc.shape, F32)

    @pl.when(ki <= qi)
    def _():
        causal = (qi * t + _lane((t, t), 0)) >= (ki * t + _lane((t, t), 1))
        v = v_ref[...]
        for a in range(2):
            sl = slice(a * LANES, (a + 1) * LANES)
            s = _dot_nt(q_ref[:, sl], k_ref[:, sl]) * scale
            s = jnp.where(causal, s, NEG_INF)
            m_prev = m_sc[a]
            m_new = jnp.maximum(m_prev, jnp.max(s, -1, keepdims=True))
            al = jnp.exp(m_prev - m_new)
            p = jnp.exp(s - m_new)
            l_sc[a] = al * l_sc[a] + jnp.sum(p, -1, keepdims=True)
            acc_sc[a] = al * acc_sc[a] + _dot(p.astype(BF16), v)
            m_sc[a] = m_new

    @pl.when(ki == pl.num_programs(3) - 1)
    def _():
        first = _lane((t, LANES), 1) < MLA_V
        o_ref[...] = jnp.where(first, acc_sc[0] / l_sc[0], acc_sc[1] / l_sc[1]).astype(o_ref.dtype)


def _mla_prompt_attn(q2d, k2d, v2d, b, s):
    t = min(MLA_TILE, s)
    nt = s // t
    hp = MLA_HEADS // 2
    return pl.pallas_call(
        functools.partial(_mla_prompt_kernel, scale=(MLA_NOPE + MLA_ROPE) ** -0.5),
        grid=(b, hp, nt, nt),
        in_specs=[pl.BlockSpec((None, t, 2 * LANES), lambda bb, h, qi, ki: (bb, qi, h)),
                  pl.BlockSpec((None, t, 2 * LANES), lambda bb, h, qi, ki: (bb, jnp.minimum(ki, qi), h)),
                  pl.BlockSpec((None, t, LANES), lambda bb, h, qi, ki: (bb, jnp.minimum(ki, qi), h))],
        out_specs=pl.BlockSpec((None, t, LANES), lambda bb, h, qi, ki: (bb, qi, h)),
        out_shape=jax.ShapeDtypeStruct((b, s, MLA_HEADS * MLA_V), BF16),
        scratch_shapes=[pltpu.VMEM((2, t, 1), F32), pltpu.VMEM((2, t, 1), F32), pltpu.VMEM((2, t, LANES), F32)],
        compiler_params=_params("arbitrary", "arbitrary", "arbitrary", "arbitrary"),
        name="mla_prompt_attn",
    )(q2d.reshape(b, s, -1), k2d.reshape(b, s, -1), v2d.reshape(b, s, -1)).reshape(b * s, -1)


def _mla_sample_kernel(pt_ref, q_ref, cn_ref, rn_ref, wabs_ref, wv_ref, *rest, npg, scale):
    c_refs, r_refs = rest[:npg], rest[npg:2 * npg]
    o_ref, qabs, rpad, m_sc, l_sc, acc_sc = rest[2 * npg:]
    c = pl.program_id(1)
    nc = pl.num_programs(1)
    t = q_ref.shape[0]
    rows = MLA_HEADS * t
    page = c_refs[0].shape[0]

    @pl.when(c == 0)
    def _():
        q = q_ref[...]
        qs = jnp.concatenate([q[:, h * LANES:(h + 1) * LANES] for h in range(MLA_HEADS)], 0)
        rh = _lane((rows, LANES), 0) // t
        qbd = jnp.concatenate([jnp.where(rh == h, qs, 0.0) for h in range(MLA_HEADS)], 1).astype(BF16)
        qabs[...] = (_dot(qbd, wabs_ref[...]) * scale).astype(BF16)
        rpad[...] = jnp.zeros(rpad.shape, F32)
        m_sc[...] = jnp.full(m_sc.shape, NEG_INF, F32)
        l_sc[...] = jnp.zeros(l_sc.shape, F32)
        acc_sc[...] = jnp.zeros(acc_sc.shape, F32)

    def step(lat, valid):
        latb = lat.astype(BF16)
        kf = jnp.concatenate([latb, rpad[...].astype(BF16)], 1)
        s = _dot_nt(qabs[...], kf)
        if valid is not None:
            s = jnp.where(valid, s, NEG_INF)
        m_prev = m_sc[...]
        m_new = jnp.maximum(m_prev, jnp.max(s, -1, keepdims=True))
        a = jnp.exp(m_prev - m_new)
        p = jnp.exp(s - m_new)
        l_sc[...] = a * l_sc[...] + jnp.sum(p, -1, keepdims=True)
        acc_sc[...] = a * acc_sc[...] + _dot(p.astype(BF16), latb)
        m_sc[...] = m_new

    for i in range(npg):
        rpad[:, :MLA_ROPE] = r_refs[i][...]
        step(c_refs[i][...], None)

    @pl.when(c == nc - 1)
    def _():
        rpad[...] = jnp.zeros(rpad.shape, F32)
        rpad[:t, :MLA_ROPE] = rn_ref[...]
        lat = jnp.concatenate([cn_ref[...], jnp.zeros((page - t, MLA_KV_RANK), F32)], 0)
        col = _lane((rows, page), 1)
        valid = col <= (_lane((rows, page), 0) % t)
        step(lat, valid)
        o_lat = (acc_sc[...] / l_sc[...]).astype(BF16)
        of = _dot(o_lat, wv_ref[...])
        n = of.shape[1]
        diag = (_lane((rows, n), 1) // MLA_V) == (_lane((rows, n), 0) // t)
        o_ref[...] = jnp.sum(jnp.where(diag, of, 0.0).reshape(MLA_HEADS, t, n), 0)


def _mla_sample_attn(q2d, ckv_new, kr_new, cache_ckv, cache_krope, slot, page_table, wabs, wv, db, t):
    page = cache_ckv.shape[2]
    n_pages = page_table.shape[1]
    npg = min(PAGES_PER_STEP, n_pages)
    assert n_pages % npg == 0
    rows = MLA_HEADS * t
    nv = MLA_HEADS * MLA_V

    def page_spec(w, i):
        return pl.BlockSpec((None, None, page, w), lambda b, c, pt: (slot, pt[b, c * npg + i], 0, 0))

    grid_spec = pltpu.PrefetchScalarGridSpec(
        num_scalar_prefetch=1,
        grid=(db, n_pages // npg),
        in_specs=[pl.BlockSpec((None, t, MLA_HEADS * LANES), lambda b, c, pt: (b, 0, 0)),
                  pl.BlockSpec((None, t, MLA_KV_RANK), lambda b, c, pt: (b, 0, 0)),
                  pl.BlockSpec((None, t, MLA_ROPE), lambda b, c, pt: (b, 0, 0)),
                  pl.BlockSpec(wabs.shape, lambda b, c, pt: (0, 0)),
                  pl.BlockSpec(wv.shape, lambda b, c, pt: (0, 0))]
        + [page_spec(MLA_KV_RANK, i) for i in range(npg)] + [page_spec(MLA_ROPE, i) for i in range(npg)],
        out_specs=pl.BlockSpec((None, t, nv), lambda b, c, pt: (b, 0, 0)),
        scratch_shapes=[pltpu.VMEM((rows, MLA_KV_RANK + LANES), BF16), pltpu.VMEM((page, LANES), F32),
                        pltpu.VMEM((rows, 1), F32), pltpu.VMEM((rows, 1), F32),
                        pltpu.VMEM((rows, MLA_KV_RANK), F32)])
    o = pl.pallas_call(
        functools.partial(_mla_sample_kernel, npg=npg, scale=(MLA_NOPE + MLA_ROPE) ** -0.5),
        grid_spec=grid_spec,
        out_shape=jax.ShapeDtypeStruct((db, t, nv), F32),
        compiler_params=_params("arbitrary", "arbitrary"),
        name="mla_sample_attn",
    )(page_table, q2d.reshape(db, t, -1), ckv_new.reshape(db, t, -1), kr_new.reshape(db, t, -1), wabs, wv,
      *([cache_ckv] * npg), *([cache_krope] * npg))
    return o.reshape(db * t, nv)


def _router_kernel(x_ref, sh_ref, sc_ref, whi_ref, wlo_ref, b_ref, h_ref, lg_ref):
    tb, ts, d = x_ref.shape
    h = (x_ref[...] * (1.0 + sc_ref[...]) + sh_ref[...]).reshape(tb * ts, d)
    hi = h.astype(BF16)
    lo = (h - hi.astype(F32)).astype(BF16)
    h_ref[...] = hi
    lg_ref[...] = _dot(hi, whi_ref[...]) + _dot(hi, wlo_ref[...]) + _dot(lo, whi_ref[...]) + b_ref[...]


def _router(x, shift, scale, whi, wlo, bias):
    b, s, d = x.shape
    tb, ts = _row_blocks(b, s)
    rows = tb * ts
    return pl.pallas_call(
        _router_kernel,
        grid=(b * s // rows,),
        in_specs=[_x_spec(tb, ts, s, d), _mod_spec(tb, ts, s, d), _mod_spec(tb, ts, s, d),
                  _resident(whi.shape), _resident(wlo.shape), _resident(bias.shape)],
        out_specs=[_rows_spec(rows, d), _rows_spec(rows, LANES)],
        out_shape=[jax.ShapeDtypeStruct((b * s, d), BF16), jax.ShapeDtypeStruct((b * s, LANES), F32)],
        compiler_params=_params("arbitrary"),
        name="router",
    )(x, shift, scale, whi, wlo, bias)


def _expert_kernel(be_ref, nb_ref, x_ref, g_ref, w1_ref, w3_ref, w2_ref, y_ref):
    i = pl.program_id(0)

    @pl.when(i < nb_ref[0])
    def _():
        x = x_ref[...]
        a = _dot(x, w1_ref[...])
        u = (a * jax.nn.sigmoid(a)) * _dot(x, w3_ref[...])
        y_ref[...] = (_dot(u.astype(BF16), w2_ref[...]) * g_ref[...]).astype(y_ref.dtype)

    @pl.when(i >= nb_ref[0])
    def _():
        y_ref[...] = jnp.zeros(y_ref.shape, y_ref.dtype)


def _experts(xb, slot_g, block_e, n_used, w1, w3, w2):
    n_slots, d = xb.shape
    ff = w1.shape[2]
    n_blocks = n_slots // MOE_ROWS
    grid_spec = pltpu.PrefetchScalarGridSpec(
        num_scalar_prefetch=2,
        grid=(n_blocks,),
        in_specs=[pl.BlockSpec((MOE_ROWS, d), lambda i, be, nb: (i, 0)),
                  pl.BlockSpec((MOE_ROWS, 1), lambda i, be, nb: (i, 0)),
                  pl.BlockSpec((None, d, ff), lambda i, be, nb: (be[i], 0, 0)),
                  pl.BlockSpec((None, d, ff), lambda i, be, nb: (be[i], 0, 0)),
                  pl.BlockSpec((None, ff, d), lambda i, be, nb: (be[i], 0, 0))],
        out_specs=pl.BlockSpec((MOE_ROWS, d), lambda i, be, nb: (i, 0)))
    return pl.pallas_call(
        _expert_kernel,
        grid_spec=grid_spec,
        out_shape=jax.ShapeDtypeStruct((n_slots, d), BF16),
        compiler_params=_params("arbitrary"),
        name="experts",
    )(block_e, n_used, xb, slot_g, w1, w3, w2)


def _combine_ln_kernel(y0_ref, y1_ref, x_ref, gate_ref, g_ref, b_ref, o_ref, *, alpha):
    tb, ts, d = x_ref.shape
    y = (y0_ref[...].astype(F32) + y1_ref[...].astype(F32)).reshape(tb, ts, d)
    z = alpha * x_ref[...] + gate_ref[...] * y
    o_ref[...] = _layernorm(z, g_ref[...], b_ref[...])


def _combine_ln(y0, y1, row_off, x, gate, g, b, alpha):
    bb, s, d = x.shape
    tb, ts = _row_blocks(bb, s)
    rows = tb * ts
    assert row_off % rows == 0
    off = row_off // rows
    return pl.pallas_call(
        functools.partial(_combine_ln_kernel, alpha=alpha),
        grid=(bb * s // rows,),
        in_specs=[_rows_spec(rows, d, off), _rows_spec(rows, d, off), _x_spec(tb, ts, s, d),
                  _mod_spec(tb, ts, s, d), _resident((1, 1, d)), _resident((1, 1, d))],
        out_specs=_x_spec(tb, ts, s, d),
        out_shape=jax.ShapeDtypeStruct(x.shape, F32),
        compiler_params=_params("arbitrary"),
        name="combine_ln",
    )(y0, y1, x, gate, g.reshape(1, 1, d), b.reshape(1, 1, d))


def _route(logits):
    n = logits.shape[0]
    lg = logits[:, :MOE_GROUPS]
    gsel = jnp.argmax(lg, -1)
    gprob = jnp.max(jax.nn.softmax(lg, -1), -1, keepdims=True)
    le = logits[:, MOE_GROUPS:MOE_GROUPS + MOE_EXPERTS].reshape(n, MOE_GROUPS, MOE_EXPERTS_PER_GROUP)
    le = jnp.take_along_axis(le, gsel[:, None, None], 1)[:, 0]
    top_v, top_i = lax.top_k(le, MOE_TOPK)
    gate = jax.nn.softmax(top_v, -1) * gprob
    eid = gsel[:, None].astype(jnp.int32) * MOE_EXPERTS_PER_GROUP + top_i.astype(jnp.int32)
    return eid, gate


def _dispatch(eid, gate):
    n = eid.shape[0]
    a = n * MOE_TOPK
    flat_e = eid.reshape(a)
    order = jnp.argsort(flat_e)
    se = flat_e[order]
    counts = jnp.bincount(flat_e, length=MOE_EXPERTS)
    padded = (counts + MOE_ROWS - 1) // MOE_ROWS * MOE_ROWS
    start = jnp.cumsum(counts) - counts
    pend = jnp.cumsum(padded)
    dest = ((pend - padded)[se] + jnp.arange(a) - start[se]).astype(jnp.int32)
    n_blocks = -(-a // MOE_ROWS) + MOE_EXPERTS
    n_slots = n_blocks * MOE_ROWS
    slot_tok = jnp.zeros((n_slots,), jnp.int32).at[dest].set((order // MOE_TOPK).astype(jnp.int32))
    slot_g = jnp.zeros((n_slots,), F32).at[dest].set(gate.reshape(a)[order])
    block_e = jnp.minimum(jnp.searchsorted(pend, jnp.arange(n_blocks) * MOE_ROWS, side='right'),
                          MOE_EXPERTS - 1).astype(jnp.int32)
    n_used = (pend[-1] // MOE_ROWS).astype(jnp.int32).reshape(1)
    pos = jnp.zeros((a,), jnp.int32).at[order].set(dest).reshape(n, MOE_TOPK)
    return slot_tok, slot_g.reshape(n_slots, 1), block_e, n_used, pos


def _moe_layer(yp, ys, mod_p, mod_s, g, b, wg, bg, we, be, w1, w3, w2, alpha):
    d = yp.shape[-1]
    npad = LANES - MOE_GROUPS - MOE_EXPERTS
    wr = jnp.concatenate([wg, we, jnp.zeros((d, npad), F32)], 1)
    whi = wr.astype(BF16)
    wlo = (wr - whi.astype(F32)).astype(BF16)
    br = jnp.concatenate([bg, be, jnp.zeros((npad,), F32)]).reshape(1, LANES)
    hp, lp = _router(yp, mod_p[3], mod_p[4], whi, wlo, br)
    hs, ls = _router(ys, mod_s[3], mod_s[4], whi, wlo, br)
    h = jnp.concatenate([hp, hs], 0)
    eid, gate = _route(jnp.concatenate([lp, ls], 0))
    slot_tok, slot_g, block_e, n_used, pos = _dispatch(eid, gate)
    yb = _experts(h[slot_tok], slot_g, block_e, n_used, w1.astype(BF16), w3.astype(BF16), w2.astype(BF16))
    y0, y1 = yb[pos[:, 0]], yb[pos[:, 1]]
    yp = _combine_ln(y0, y1, 0, yp, mod_p[5], g, b, alpha)
    ys = _combine_ln(y0, y1, hp.shape[0], ys, mod_s[5], g, b, alpha)
    return yp, ys


def kernel(x_prompt, x_sample, cache_diff_k, cache_diff_v, cache_dil_kv_g0, cache_dil_kv_g1, cache_dil_kv_g2, cache_mla_ckv, cache_mla_krope, page_table, c_prompt, c_sample, rel_bias, ada_w, ada_b, ln_g, ln_b, diff_w_in, diff_lam, diff_subln, diff_w_out, dil_w_in, dil_w_out, mla_w_in, mla_q_norm, mla_kv_norm, mla_w_uq, mla_w_uk, mla_w_uv, mla_w_out, moe_wg, moe_bg, moe_we, moe_be, moe_w1, moe_w3, moe_w2):
    depth = ada_w.shape[0]
    b, s, d = x_prompt.shape
    db, t, _ = x_sample.shape
    past = page_table.shape[1] * cache_diff_k.shape[2]
    alpha = (2 * depth) ** 0.25
    dil_cache = (cache_dil_kv_g0, cache_dil_kv_g1, cache_dil_kv_g2)
    bias_d = _bias_by_dist(rel_bias, max(s, past + t))

    mods = _ada(jnp.concatenate([c_prompt, c_sample], 0), ada_w, ada_b)
    yp, ys = x_prompt, x_sample
    dk_p, dv_p, dk_s, dv_s = [], [], [], []
    dl_p = [[] for _ in DIL_PATTERNS]
    dl_s = [[] for _ in DIL_PATTERNS]
    ck_p, kr_p, ck_s, kr_s = [], [], [], []
    for i in range(depth):
        kind, j = i % N_MIXERS, i // N_MIXERS
        mod_p = [mods[i, :b, None, m * d:(m + 1) * d] for m in range(6)]
        mod_s = [mods[i, b:, None, m * d:(m + 1) * d] for m in range(6)]
        if kind == 0:
            nq = DIFF_HEADS * 2 * DIFF_DH
            nk = DIFF_KV_HEADS * 2 * DIFF_DH
            splits = ((0, nq), (nq, nq + nk), (nq + nk, nq + 2 * nk))
            w_in = diff_w_in[j].astype(BF16)
            lam_init = 0.8 - 0.6 * math.exp(-0.3 * i)
            lp = diff_lam[j].astype(F32)
            lam = jnp.exp(jnp.sum(lp[0] * lp[1])) - jnp.exp(jnp.sum(lp[2] * lp[3])) + lam_init
            qp, kp, vp = _proj(yp, mod_p[0], mod_p[1], w_in, splits, (BF16, F32, F32))
            qs, ks, vs = _proj(ys, mod_s[0], mod_s[1], w_in, splits, (F32, F32, F32))
            o_p = _diff_prompt_attn(qp, kp, vp, bias_d, lam, lam_init, diff_subln[j], b, s)
            o_s = _diff_sample_attn(qs, ks, vs, cache_diff_k, cache_diff_v, j, page_table, bias_d, lam, lam_init,
                                    diff_subln[j], db, t)
            dk_p.append(kp.reshape(b, s, DIFF_KV_HEADS, 2 * DIFF_DH))
            dv_p.append(vp.reshape(b, s, DIFF_KV_HEADS, 2 * DIFF_DH))
            dk_s.append(ks.reshape(db, t, DIFF_KV_HEADS, 2 * DIFF_DH))
            dv_s.append(vs.reshape(db, t, DIFF_KV_HEADS, 2 * DIFF_DH))
            w_out = diff_w_out[j].astype(BF16)
            yp = _outproj_ln(o_p, yp, mod_p[2], w_out, ln_g[i, 0], ln_b[i, 0], alpha)
            ys = _outproj_ln(o_s, ys, mod_s[2], w_out, ln_g[i, 0], ln_b[i, 0], alpha)
        elif kind == 1:
            hw = DIL_HEADS * DIL_DH
            splits, dts_p, dts_s = [], [], []
            for g in range(len(DIL_PATTERNS)):
                splits += [(3 * g * hw, (3 * g + 1) * hw), ((3 * g + 1) * hw, (3 * g + 3) * hw)]
                dts_p += [BF16, F32]
                dts_s += [F32, F32]
            w_in = dil_w_in[j].astype(BF16)
            pp = _proj(yp, mod_p[0], mod_p[1], w_in, splits, dts_p)
            ps = _proj(ys, mod_s[0], mod_s[1], w_in, splits, dts_s)
            outs, lses = [], []
            for g, (win, dil) in enumerate(DIL_PATTERNS):
                o_g, lse_g = _dil_prompt_group(pp[2 * g], pp[2 * g + 1], bias_d, win, dil, b, s)
                outs.append(o_g)
                lses.append(lse_g)
                kv = pp[2 * g + 1].reshape(b, s, 2, DIL_HEADS, DIL_DH)
                dl_p[g].append(kv[:, s - min(win, s):])
            bufs = [cb[j] for cb in dil_cache]
            kv_news = [ps[2 * g + 1] for g in range(len(DIL_PATTERNS))]
            o_s = _dil_sample_attn([ps[2 * g] for g in range(len(DIL_PATTERNS))], kv_news, bufs, bias_d, past, db, t)
            for g in range(len(DIL_PATTERNS)):
                new = kv_news[g].reshape(db, t, 2, DIL_HEADS, DIL_DH).astype(bufs[g].dtype)
                dl_s[g].append(jnp.concatenate([bufs[g], new], 1)[:, -bufs[g].shape[1]:])
            w_out = dil_w_out[j].astype(BF16)
            yp = _merge_outproj_ln(outs, lses, yp, mod_p[2], w_out, ln_g[i, 0], ln_b[i, 0], alpha)
            ys = _outproj_ln(o_s, ys, mod_s[2], w_out, ln_g[i, 0], ln_b[i, 0], alpha)
        else:
            win, wqa, wqb, wk, wv, wabs = _mla_weights(mla_w_in[j], mla_w_uq[j], mla_w_uk[j], mla_w_uv[j])
            ckv_p, krp, qp, kfp, vp = _mla_proj(yp, mod_p[0], mod_p[1], jnp.arange(s), win, mla_q_norm[j],
                                                mla_kv_norm[j], wqa, wqb, wk, wv, True)
            ckv_s, krs, qs = _mla_proj(ys, mod_s[0], mod_s[1], past + jnp.arange(t), win, mla_q_norm[j],
                                       mla_kv_norm[j], wqa, wqb, wk, wv, False)
            o_p = _mla_prompt_attn(qp, kfp, vp, b, s)
            o_s = _mla_sample_attn(qs, ckv_s, krs, cache_mla_ckv, cache_mla_krope, j, page_table, wabs, wv, db, t)
            ck_p.append(ckv_p.reshape(b, s, MLA_KV_RANK))
            kr_p.append(krp.reshape(b, s, MLA_ROPE))
            ck_s.append(ckv_s.reshape(db, t, MLA_KV_RANK))
            kr_s.append(krs.reshape(db, t, MLA_ROPE))
            w_out = mla_w_out[j].astype(BF16)
            yp = _outproj_ln(o_p, yp, mod_p[2], w_out, ln_g[i, 0], ln_b[i, 0], alpha)
            ys = _outproj_ln(o_s, ys, mod_s[2], w_out, ln_g[i, 0], ln_b[i, 0], alpha)
        yp, ys = _moe_layer(yp, ys, mod_p, mod_s, ln_g[i, 1], ln_b[i, 1], moe_wg[i], moe_bg[i], moe_we[i],
                            moe_be[i], moe_w1[i], moe_w3[i], moe_w2[i], alpha)
    return (yp, ys, jnp.stack(dk_p), jnp.stack(dv_p), jnp.stack(dk_s), jnp.stack(dv_s),
            jnp.stack(dl_p[0]), jnp.stack(dl_p[1]), jnp.stack(dl_p[2]),
            jnp.stack(dl_s[0]), jnp.stack(dl_s[1]), jnp.stack(dl_s[2]),
            jnp.stack(ck_p), jnp.stack(kr_p), jnp.stack(ck_s), jnp.stack(kr_s))
```

```python
import functools
import math

import jax
import jax.numpy as jnp
from jax import lax
from jax.experimental import pallas as pl
from jax.experimental.pallas import tpu as pltpu

F32 = jnp.float32
BF16 = jnp.bfloat16

N_MIXERS = 3
N_BUCKETS = 32
REL_MAX_DIST = 2048
DIFF_DH = 64
DIFF_HEADS = 8
DIFF_KV_HEADS = 2
DIL_PATTERNS = ((128, 1), (512, 4), (2048, 16))
DIL_HEADS = 8
DIL_DH = 64
MLA_HEADS = 16
MLA_NOPE = 64
MLA_ROPE = 32
MLA_V = 64
MLA_Q_RANK = 384
MLA_KV_RANK = 256
ROPE_BASE = 10000.0
MOE_GROUPS = 4
MOE_EXPERTS_PER_GROUP = 8
MOE_EXPERTS = MOE_GROUPS * MOE_EXPERTS_PER_GROUP
MOE_TOPK = 2
LN_EPS = 1e-5
NEG_INF = -1e30
LOG2E = math.log2(math.e)

LANES = 128
ROW_TILE = 512
MOE_ROWS = 256
DIFF_TILE = 256
MLA_TILE = 512
DIL_TILE = 128
PAGES_PER_STEP = 16
VMEM_LIMIT_BYTES = 56 * 1024 * 1024


def _params(*sem):
    return pltpu.CompilerParams(dimension_semantics=sem, vmem_limit_bytes=VMEM_LIMIT_BYTES)


def _dot(a, b):
    return jnp.dot(a, b, preferred_element_type=F32)


def _dot_nt(a, b):
    return lax.dot_general(a, b, (((1,), (1,)), ((), ())), preferred_element_type=F32)


def _resident(shape):
    nd = len(shape)
    return pl.BlockSpec(shape, lambda *_: (0,) * nd)


def _lane(shape, axis):
    return lax.broadcasted_iota(jnp.int32, shape, axis)


def _rel_bucket(dist):
    max_exact = N_BUCKETS // 2
    n = jnp.maximum(dist, 0)
    nf = jnp.maximum(n, 1).astype(F32)
    large = max_exact + (jnp.log(nf / max_exact) / math.log(REL_MAX_DIST / max_exact)
                         * (N_BUCKETS - max_exact)).astype(jnp.int32)
    return jnp.where(n < max_exact, n, jnp.minimum(large, N_BUCKETS - 1))


def _bias_by_dist(rel_bias, n):
    hit = _rel_bucket(jnp.arange(n))[None, :, None] == jnp.arange(N_BUCKETS)[None, None, :]
    return jnp.sum(jnp.where(hit, rel_bias.astype(F32).T[:, None, :], 0.0), -1)


def _toeplitz(w, rows, cols):
    p = rows + cols
    u = jnp.concatenate([w[..., :cols][..., ::-1], w[..., :1], w[..., cols:][..., ::-1]], -1)
    t = jnp.tile(u, (1,) * (w.ndim - 1) + (rows,))[..., :rows * (p - 1)]
    return t.reshape(w.shape[:-1] + (rows, p - 1))[..., :cols]


def _neg(shape):
    return jnp.full(shape, NEG_INF, F32)


def _rope_tables(pos):
    half = MLA_ROPE // 2
    inv = ROPE_BASE ** (-jnp.arange(half, dtype=F32) / half)
    ang = pos.astype(F32)[:, None] * inv[None, :]
    cos, sin = jnp.cos(ang), jnp.sin(ang)
    n = pos.shape[0]
    cq = jnp.concatenate([jnp.ones((n, MLA_NOPE), F32), cos, cos, jnp.zeros((n, 32), F32)], 1)
    sq = jnp.concatenate([jnp.zeros((n, MLA_NOPE), F32), -sin, sin, jnp.zeros((n, 32), F32)], 1)
    ck = jnp.concatenate([cos, cos, jnp.zeros((n, LANES - MLA_ROPE), F32)], 1)
    sk = jnp.concatenate([-sin, sin, jnp.zeros((n, LANES - MLA_ROPE), F32)], 1)
    return cq[None], sq[None], ck[None], sk[None]


def _ada_kernel(c_ref, w_ref, b_ref, o_ref):
    c = c_ref[...]
    a = (c * jax.nn.sigmoid(c)).astype(BF16)
    o_ref[...] = _dot(a, w_ref[...].astype(BF16)) + b_ref[...]


def _ada(c_all, ada_w, ada_b):
    depth, d, n = ada_w.shape
    m = c_all.shape[0]
    tn = 1536
    return pl.pallas_call(
        _ada_kernel,
        grid=(depth, n // tn),
        in_specs=[pl.BlockSpec((m, d), lambda l, j: (0, 0)),
                  pl.BlockSpec((None, d, tn), lambda l, j: (l, 0, j)),
                  pl.BlockSpec((None, 1, tn), lambda l, j: (l, 0, j))],
        out_specs=pl.BlockSpec((None, m, tn), lambda l, j: (l, 0, j)),
        out_shape=jax.ShapeDtypeStruct((depth, m, n), F32),
        compiler_params=_params("arbitrary", "arbitrary"),
        name="ada",
    )(c_all, ada_w, ada_b.reshape(depth, 1, n))


def _row_blocks(b, s):
    if s >= ROW_TILE:
        assert s % ROW_TILE == 0
        return 1, ROW_TILE
    tb = min(b, max(1, ROW_TILE // s))
    assert b % tb == 0
    return tb, s


def _x_spec(tb, ts, s, d):
    nst = s // ts
    return pl.BlockSpec((tb, ts, d), lambda i: (i // nst, i % nst, 0))


def _mod_spec(tb, ts, s, d):
    nst = s // ts
    return pl.BlockSpec((tb, 1, d), lambda i: (i // nst, 0, 0))


def _pos_spec(ts, s, c):
    nst = s // ts
    return pl.BlockSpec((1, ts, c), lambda i: (0, i % nst, 0))


def _rows_spec(rows, n, off=0):
    return pl.BlockSpec((rows, n), lambda i: (i + off, 0))


def _proj_kernel(x_ref, sh_ref, sc_ref, w_ref, *rest, splits, transposed):
    wt_ref = rest[0] if transposed else None
    o_refs = rest[1:] if transposed else rest
    tb, ts, d = x_ref.shape
    h = x_ref[...] * (1.0 + sc_ref[...]) + sh_ref[...]
    h = h.reshape(tb * ts, d).astype(BF16)
    for o_ref, (c0, c1, mul) in zip(o_refs, splits):
        for s in range(c0, c1, 512):
            e = min(s + 512, c1)
            o_ref[:, s - c0:e - c0] = (_dot(h, w_ref[:, s:e]) * mul).astype(o_ref.dtype)
    if transposed:
        o_refs[-1][...] = _dot_nt(wt_ref[...], h).astype(o_refs[-1].dtype)


def _proj(x, shift, scale, w, splits, dtypes, wt=None):
    b, s, d = x.shape
    tb, ts = _row_blocks(b, s)
    rows = tb * ts
    nst = s // ts
    in_specs = [_x_spec(tb, ts, s, d), _mod_spec(tb, ts, s, d), _mod_spec(tb, ts, s, d), _resident(w.shape)]
    out_specs = [_rows_spec(rows, c1 - c0) for (c0, c1, _) in splits]
    out_shape = [jax.ShapeDtypeStruct((b * s, c1 - c0), dt) for (c0, c1, _), dt in zip(splits, dtypes)]
    args = [x, shift, scale, w]
    if wt is not None:
        assert tb == 1
        in_specs.append(_resident(wt.shape))
        args.append(wt)
        out_specs.append(pl.BlockSpec((None, wt.shape[0], ts), lambda i: (i // nst, 0, i % nst)))
        out_shape.append(jax.ShapeDtypeStruct((b, wt.shape[0], s), BF16))
    return pl.pallas_call(
        functools.partial(_proj_kernel, splits=tuple(splits), transposed=wt is not None),
        grid=(b * s // rows,),
        in_specs=in_specs, out_specs=out_specs, out_shape=out_shape,
        compiler_params=_params("arbitrary"),
        name="proj",
    )(*args)


def _layernorm(z, g, b):
    mu = jnp.mean(z, -1, keepdims=True)
    zc = z - mu
    var = jnp.mean(zc * zc, -1, keepdims=True)
    return zc * lax.rsqrt(var + LN_EPS) * g + b


def _outproj_ln_kernel(o_ref, x_ref, gate_ref, w_ref, g_ref, b_ref, y_ref, *, alpha):
    tb, ts, d = x_ref.shape
    f = _dot(o_ref[...].astype(BF16), w_ref[...]).reshape(tb, ts, d)
    z = alpha * x_ref[...] + gate_ref[...] * f
    y_ref[...] = _layernorm(z, g_ref[...], b_ref[...])


def _outproj_ln(o2d, x, gate, w, g, b, alpha):
    bb, s, d = x.shape
    tb, ts = _row_blocks(bb, s)
    rows = tb * ts
    return pl.pallas_call(
        functools.partial(_outproj_ln_kernel, alpha=alpha),
        grid=(bb * s // rows,),
        in_specs=[_rows_spec(rows, o2d.shape[1]), _x_spec(tb, ts, s, d), _mod_spec(tb, ts, s, d),
                  _resident(w.shape), _resident((1, 1, d)), _resident((1, 1, d))],
        out_specs=_x_spec(tb, ts, s, d),
        out_shape=jax.ShapeDtypeStruct(x.shape, F32),
        compiler_params=_params("arbitrary"),
        name="outproj_ln",
    )(o2d, x, gate, w, g.reshape(1, 1, d), b.reshape(1, 1, d))


def _merge_outproj_ln_kernel(o0, o1, o2, l0, l1, l2, x_ref, gate_ref, w_ref, g_ref, b_ref, y_ref, *, alpha):
    tb, ts, d = x_ref.shape
    a0, a1, a2 = l0[...], l1[...], l2[...]
    mx = jnp.maximum(jnp.maximum(a0, a1), a2)
    e0, e1, e2 = jnp.exp(a0 - mx), jnp.exp(a1 - mx), jnp.exp(a2 - mx)
    o = (e0 * o0[...] + e1 * o1[...] + e2 * o2[...]) / (e0 + e1 + e2)
    f = _dot(o.astype(BF16), w_ref[...]).reshape(tb, ts, d)
    z = alpha * x_ref[...] + gate_ref[...] * f
    y_ref[...] = _layernorm(z, g_ref[...], b_ref[...])


def _merge_outproj_ln(outs, lses, x, gate, w, g, b, alpha):
    bb, s, d = x.shape
    tb, ts = _row_blocks(bb, s)
    rows = tb * ts
    k = outs[0].shape[1]
    return pl.pallas_call(
        functools.partial(_merge_outproj_ln_kernel, alpha=alpha),
        grid=(bb * s // rows,),
        in_specs=[_rows_spec(rows, k)] * 6 + [_x_spec(tb, ts, s, d), _mod_spec(tb, ts, s, d),
                                              _resident(w.shape), _resident((1, 1, d)), _resident((1, 1, d))],
        out_specs=_x_spec(tb, ts, s, d),
        out_shape=jax.ShapeDtypeStruct(x.shape, F32),
        compiler_params=_params("arbitrary"),
        name="merge_outproj_ln",
    )(*outs, *lses, x, gate, w, g.reshape(1, 1, d), b.reshape(1, 1, d))


def _softmax_step(s_blocks, vt_blocks, m_sc, l_sc, acc_sc, i):
    m_prev = m_sc[i]
    m_new = m_prev
    for s in s_blocks:
        m_new = jnp.maximum(m_new, jnp.max(s, 0, keepdims=True))
    a = jnp.exp2(m_prev - m_new)
    l = a * l_sc[i]
    acc = a * acc_sc[i]
    for s, v_t in zip(s_blocks, vt_blocks):
        p = jnp.exp2(s - m_new)
        l = l + jnp.sum(p, 0, keepdims=True)
        acc = acc + _dot(v_t, p.astype(BF16))
    l_sc[i] = l
    acc_sc[i] = acc
    m_sc[i] = m_new


def _causal_pairs(nt):
    pairs = [(qi, ki) for qi in range(nt) for ki in range(qi + 1)]
    return (jnp.asarray([p[0] for p in pairs], jnp.int32), jnp.asarray([p[1] for p in pairs], jnp.int32))


def _init_stats(m_sc, l_sc, acc_sc):
    m_sc[...] = jnp.full(m_sc.shape, NEG_INF, F32)
    l_sc[...] = jnp.zeros(l_sc.shape, F32)
    acc_sc[...] = jnp.zeros(acc_sc.shape, F32)


def _diff_prompt_kernel(qi_ref, ki_ref, lam_ref, q_ref, k_ref, vt_ref, bias_ref, subln_ref, o_ref, m_sc, l_sc, acc_sc,
                        *, post):
    pair = pl.program_id(2)
    qi, ki = qi_ref[pair], ki_ref[pair]
    r_heads = q_ref.shape[1] // LANES

    @pl.when(ki == 0)
    def _():
        _init_stats(m_sc, l_sc, acc_sc)

    k = k_ref[...]
    v_t = vt_ref[...]
    first = _lane(k.shape, 1) < DIFF_DH
    tk = k.shape[0]
    k2 = jnp.concatenate([jnp.where(first, k, jnp.zeros_like(k)), jnp.where(first, jnp.zeros_like(k), k)], 0)
    scores = [_dot_nt(k2, q_ref[:, r * LANES:(r + 1) * LANES]) for r in range(r_heads)]
    for r in range(r_heads):
        bias = bias_ref[r, qi - ki]
        for mi in range(2):
            _softmax_step([scores[r][mi * tk:(mi + 1) * tk] + bias], [v_t], m_sc, l_sc, acc_sc, 2 * r + mi)

    @pl.when(ki == qi)
    def _():
        lam = lam_ref[0, 0]
        for r in range(r_heads):
            o = acc_sc[2 * r] * (1.0 / l_sc[2 * r]) - lam * (acc_sc[2 * r + 1] * (1.0 / l_sc[2 * r + 1]))
            o = o * lax.rsqrt(jnp.mean(o * o, 0, keepdims=True) + LN_EPS) * subln_ref[...] * post
            o_ref[:, r * LANES:(r + 1) * LANES] = o.T.astype(o_ref.dtype)


def _diff_prompt_attn(q2d, kb2d, v_t, bias_d, lam, lam_init, subln, b, s):
    t = min(DIFF_TILE, s)
    nt = s // t
    r = DIFF_HEADS // DIFF_KV_HEADS
    gx = jnp.concatenate([_neg((DIFF_HEADS, t)), bias_d[:, :nt * t]], 1)
    win = jnp.stack([gx[:, d * t + 1:d * t + 2 * t] for d in range(nt)], 1)[..., ::-1]
    tiles = _toeplitz(win, t, t) * LOG2E
    hw = r * LANES
    qis, kis = _causal_pairs(nt)
    grid_spec = pltpu.PrefetchScalarGridSpec(
        num_scalar_prefetch=2,
        grid=(DIFF_KV_HEADS, b, qis.shape[0]),
        in_specs=[pl.BlockSpec(memory_space=pltpu.SMEM),
                  pl.BlockSpec((None, t, hw), lambda g, bb, p, qa, ka: (bb, qa[p], g)),
                  pl.BlockSpec((None, t, LANES), lambda g, bb, p, qa, ka: (bb, ka[p], g)),
                  pl.BlockSpec((None, LANES, t), lambda g, bb, p, qa, ka: (bb, g, ka[p])),
                  pl.BlockSpec((r, nt, t, t), lambda g, bb, p, qa, ka: (g, 0, 0, 0)),
                  pl.BlockSpec((LANES, 1), lambda g, bb, p, qa, ka: (0, 0))],
        out_specs=pl.BlockSpec((None, t, hw), lambda g, bb, p, qa, ka: (bb, qa[p], g)),
        scratch_shapes=[pltpu.VMEM((2 * r, 1, t), F32), pltpu.VMEM((2 * r, 1, t), F32),
                        pltpu.VMEM((2 * r, LANES, t), F32)])
    return pl.pallas_call(
        functools.partial(_diff_prompt_kernel, post=1.0 - lam_init),
        grid_spec=grid_spec,
        out_shape=jax.ShapeDtypeStruct((b, s, DIFF_HEADS * LANES), BF16),
        compiler_params=_params("arbitrary", "arbitrary", "arbitrary"),
        name="diff_prompt_attn",
    )(qis, kis, lam.reshape(1, 1), q2d.reshape(b, s, -1), kb2d.reshape(b, s, -1), v_t, tiles,
      subln.reshape(LANES, 1)).reshape(b * s, -1)


def _diff_sample_kernel(pt_ref, lam_ref, q_ref, kn_ref, vn_ref, bias_ref, subln_ref, *rest, npg, qmul, post):
    k_refs, v_refs = rest[:npg], rest[npg:2 * npg]
    o_ref, qbd, eye, m_sc, l_sc, acc_sc = rest[2 * npg:]
    c = pl.program_id(1)
    nc = pl.num_programs(1)
    t = q_ref.shape[0]

    @pl.when(c == 0)
    def _():
        q = q_ref[...] * qmul
        first = _lane((t, LANES), 1) < DIFF_DH
        zero = jnp.zeros((t, LANES), F32)
        heads = [q[:, h * LANES:(h + 1) * LANES] for h in range(DIFF_HEADS)]
        pieces = [jnp.where(first, qh, zero) for qh in heads] + [jnp.where(first, zero, qh) for qh in heads]
        qbd[...] = jnp.concatenate(pieces, 0).astype(BF16)
        eye[...] = (_lane(eye.shape, 0) == _lane(eye.shape, 1)).astype(BF16)
        _init_stats(m_sc, l_sc, acc_sc)

    def update(pages):
        scores = [_dot_nt(kp.astype(BF16), qbd[...]) + bias for kp, _, bias in pages]
        v_ts = [_dot_nt(eye[...], vp.astype(BF16)).astype(BF16) for _, vp, _ in pages]
        _softmax_step(scores, v_ts, m_sc, l_sc, acc_sc, 0)

    update([(k_refs[i][...], v_refs[i][...], bias_ref[c * npg + i]) for i in range(npg)])

    @pl.when(c == nc - 1)
    def _():
        pad = jnp.zeros((k_refs[0].shape[0] - kn_ref.shape[0], LANES), F32)
        update([(jnp.concatenate([kn_ref[...], pad], 0), jnp.concatenate([vn_ref[...], pad], 0),
                 bias_ref[nc * npg])])
        o = acc_sc[0] * (1.0 / l_sc[0])
        o = o - lam_ref[0, 0] * pltpu.roll(o, o.shape[1] // 2, 1)
        o_ref[...] = o * lax.rsqrt(jnp.mean(o * o, 0, keepdims=True) + LN_EPS) * subln_ref[...] * post


def _diff_sample_attn(q2d, k2d, v2d, cache_k, cache_v, slot, page_table, bias_d, lam, lam_init, subln, db, t):
    n_phys, page = cache_k.shape[1], cache_k.shape[2]
    n_pages = page_table.shape[1]
    past = n_pages * page
    npg = min(PAGES_PER_STEP, n_pages)
    assert n_pages % npg == 0
    prow = page * DIFF_KV_HEADS
    ck = cache_k.reshape(cache_k.shape[0], n_phys, prow, LANES)
    cv = cache_v.reshape(cache_v.shape[0], n_phys, prow, LANES)
    cols = 2 * DIFF_HEADS * t
    assert cols == LANES
    npos = (n_pages + 1) * page
    f = _toeplitz(jnp.concatenate([_neg((DIFF_HEADS, page - 1)), bias_d[:, :past + t]], 1), t, npos)
    f = f.reshape(DIFF_HEADS, t, n_pages + 1, page).transpose(2, 3, 0, 1)
    own = (jnp.arange(DIFF_KV_HEADS)[:, None] == (jnp.arange(DIFF_HEADS) // (DIFF_HEADS // DIFF_KV_HEADS))[None, :])
    table = jnp.where(own[None, None, :, None, :, None], f[:, :, None, None, :, :] * LOG2E, NEG_INF)
    table = jnp.broadcast_to(table, (n_pages + 1, page, DIFF_KV_HEADS, 2, DIFF_HEADS, t))
    table = table.reshape(n_pages + 1, prow, cols)

    def page_spec(i):
        return pl.BlockSpec((None, None, prow, LANES), lambda b, c, pt: (slot, pt[b, c * npg + i], 0, 0))

    grid_spec = pltpu.PrefetchScalarGridSpec(
        num_scalar_prefetch=1,
        grid=(db, n_pages // npg),
        in_specs=[pl.BlockSpec(memory_space=pltpu.SMEM),
                  pl.BlockSpec((None, t, DIFF_HEADS * LANES), lambda b, c, pt: (b, 0, 0)),
                  pl.BlockSpec((None, t * DIFF_KV_HEADS, LANES), lambda b, c, pt: (b, 0, 0)),
                  pl.BlockSpec((None, t * DIFF_KV_HEADS, LANES), lambda b, c, pt: (b, 0, 0)),
                  pl.BlockSpec(table.shape, lambda b, c, pt: (0, 0, 0)),
                  pl.BlockSpec((LANES, 1), lambda b, c, pt: (0, 0))]
        + [page_spec(i) for i in range(npg)] + [page_spec(i) for i in range(npg)],
        out_specs=pl.BlockSpec((None, LANES, cols), lambda b, c, pt: (b, 0, 0)),
        scratch_shapes=[pltpu.VMEM((cols, LANES), BF16), pltpu.VMEM((LANES, LANES), BF16),
                        pltpu.VMEM((1, 1, cols), F32), pltpu.VMEM((1, 1, cols), F32),
                        pltpu.VMEM((1, LANES, cols), F32)])
    o = pl.pallas_call(
        functools.partial(_diff_sample_kernel, npg=npg, qmul=DIFF_DH ** -0.5 * LOG2E, post=1.0 - lam_init),
        grid_spec=grid_spec,
        out_shape=jax.ShapeDtypeStruct((db, LANES, cols), F32),
        compiler_params=_params("arbitrary", "arbitrary"),
        name="diff_sample_attn",
    )(page_table, lam.reshape(1, 1), q2d.reshape(db, t, -1), k2d.reshape(db, t * DIFF_KV_HEADS, LANES),
      v2d.reshape(db, t * DIFF_KV_HEADS, LANES), table, subln.reshape(LANES, 1),
      *([ck] * npg), *([cv] * npg))
    o = o[:, :, :cols // 2].reshape(db, LANES, DIFF_HEADS, t)
    return o.transpose(0, 3, 2, 1).reshape(db * t, DIFF_HEADS * LANES)


def _dil_prompt_kernel(q_ref, kp_ref, kc_ref, vp_ref, vc_ref, bias_ref, o_ref, lse_ref, *, scale):
    qi = pl.program_id(2)
    tq = q_ref.shape[0]
    q = q_ref[...]
    k = jnp.concatenate([kp_ref[...], kc_ref[...]], 0).astype(BF16)
    v = jnp.concatenate([vp_ref[...], vc_ref[...]], 0).astype(BF16)
    col = _lane((tq, 2 * tq), 1)
    prev_ok = (col >= tq) | (qi > 0)
    first = _lane((tq, LANES), 1) < DIL_DH
    for p in range(DIL_HEADS // 2):
        sl = slice(p * LANES, (p + 1) * LANES)
        qp, kp, vp = q[:, sl], k[:, sl], v[:, sl]
        outs, lses = [], []
        for a in range(2):
            qa = jnp.where(first, qp, jnp.zeros_like(qp)) if a == 0 else jnp.where(first, jnp.zeros_like(qp), qp)
            s = _dot_nt(qa, kp) * scale + bias_ref[2 * p + a]
            s = jnp.where(prev_ok, s, NEG_INF)
            m = jnp.max(s, -1, keepdims=True)
            e = jnp.exp(s - m)
            l = jnp.sum(e, -1, keepdims=True)
            outs.append(_dot(e.astype(BF16), vp) / l)
            lses.append(m + jnp.log(l))
        o_ref[:, sl] = jnp.where(first, outs[0], outs[1])
        lse_ref[:, sl] = jnp.where(first, jnp.broadcast_to(lses[0], (tq, LANES)),
                                   jnp.broadcast_to(lses[1], (tq, LANES)))


def _dil_prompt_group(q2d, kv2d, bias_d, window, dil, b, s):
    ln = s // dil
    band = window // dil
    tq = min(DIL_TILE, ln)
    assert band <= tq or ln == tq
    nb = ln // tq
    hw = DIL_HEADS * DIL_DH
    wvec = jnp.concatenate([_neg((DIL_HEADS, tq - 1)), bias_d[:, :band * dil + 1:dil],
                            _neg((DIL_HEADS, 2 * tq - band - 1))], 1)
    table = _toeplitz(wvec, tq, 2 * tq)
    qv = q2d.reshape(b, ln, dil * hw)
    kvv = kv2d.reshape(b, ln, dil * 2 * hw)

    def kv_spec(prev, val):
        return pl.BlockSpec((None, tq, hw),
                            lambda bb, c, qi: (bb, jnp.maximum(qi - 1, 0) if prev else qi, 2 * c + val))

    o, lse = pl.pallas_call(
        functools.partial(_dil_prompt_kernel, scale=DIL_DH ** -0.5),
        grid=(b, dil, nb),
        in_specs=[pl.BlockSpec((None, tq, hw), lambda bb, c, qi: (bb, qi, c)),
                  kv_spec(True, 0), kv_spec(False, 0), kv_spec(True, 1), kv_spec(False, 1),
                  pl.BlockSpec(table.shape, lambda bb, c, qi: (0, 0, 0))],
        out_specs=[pl.BlockSpec((None, tq, hw), lambda bb, c, qi: (bb, qi, c))] * 2,
        out_shape=[jax.ShapeDtypeStruct((b, ln, dil * hw), F32)] * 2,
        compiler_params=_params("arbitrary", "arbitrary", "arbitrary"),
        name="dil_prompt_attn",
    )(qv, kvv, kvv, kvv, kvv, table)
    return o.reshape(b * s, hw), lse.reshape(b * s, hw)


def _dil_sample_kernel(q0, q1, q2, kv0, kv1, kv2, b0, b1, b2, o_ref, *, scale):
    t = q0.shape[0]
    hw = DIL_HEADS * DIL_DH
    rows = DIL_HEADS * t
    diag = (_lane((rows, hw), 1) // DIL_DH) == (_lane((rows, hw), 0) // t)
    outs, lses = [], []
    for q_ref, kv_ref, b_ref in ((q0, kv0, b0), (q1, kv1, b1), (q2, kv2, b2)):
        q = q_ref[...]
        qbd = jnp.where(diag, jnp.concatenate([q] * DIL_HEADS, 0), 0.0).astype(BF16)
        s = _dot_nt(qbd, kv_ref[:, :hw]) * scale + b_ref[...]
        m = jnp.max(s, -1, keepdims=True)
        e = jnp.exp(s - m)
        l = jnp.sum(e, -1, keepdims=True)
        outs.append(_dot(e.astype(BF16), kv_ref[:, hw:]) / l)
        lses.append(m + jnp.log(l))
    mx = jnp.maximum(jnp.maximum(lses[0], lses[1]), lses[2])
    es = [jnp.exp(x - mx) for x in lses]
    o = (es[0] * outs[0] + es[1] * outs[1] + es[2] * outs[2]) / (es[0] + es[1] + es[2])
    o = jnp.where(diag, o, 0.0)
    o_ref[...] = jnp.sum(o.reshape(DIL_HEADS, t, hw), 0)


def _dil_sample_attn(qs, kv_news, bufs, bias_d, past, db, t):
    hw = DIL_HEADS * DIL_DH
    rows = DIL_HEADS * t
    kvs, tables = [], []
    for (window, dil), kvn, buf in zip(DIL_PATTERNS, kv_news, bufs):
        wb = buf.shape[1]
        pad = -(wb + t) % LANES
        kvs.append(jnp.concatenate([buf.reshape(db, wb, 2 * hw).astype(BF16),
                                    kvn.reshape(db, t, 2 * hw).astype(BF16),
                                    jnp.zeros((db, pad, 2 * hw), BF16)], 1))
        wp = wb + t + pad
        f = _toeplitz(jnp.concatenate([_neg((DIL_HEADS, wp - 1 - wb)), bias_d[:, :wb + t]], 1), t, wp)
        delta = wb + jnp.arange(t)[:, None] - jnp.arange(wp)[None, :]
        ok = (delta >= 0) & (delta % dil == 0) & (delta <= window) & (past - delta + jnp.arange(t)[:, None] >= 0)
        tables.append(jnp.where(ok[None], f, NEG_INF).reshape(rows, wp))
    o = pl.pallas_call(
        functools.partial(_dil_sample_kernel, scale=DIL_DH ** -0.5),
        grid=(db,),
        in_specs=[pl.BlockSpec((None, t, hw), lambda b: (b, 0, 0))] * 3
        + [pl.BlockSpec((None,) + kv.shape[1:], lambda b: (b, 0, 0)) for kv in kvs]
        + [_resident(tb.shape) for tb in tables],
        out_specs=pl.BlockSpec((None, t, hw), lambda b: (b, 0, 0)),
        out_shape=jax.ShapeDtypeStruct((db, t, hw), F32),
        compiler_params=_params("arbitrary"),
        name="dil_sample_attn",
    )(*[q.reshape(db, t, hw) for q in qs], *kvs, *tables)
    return o.reshape(db * t, hw)


def _mla_proj_kernel(x_ref, sh_ref, sc_ref, cq_ref, sq_ref, ck_ref, sk_ref, win_ref, qn_ref, kvn_ref,
                     wqa_ref, wqb_ref, *rest, with_kv, qmul):
    if with_kv:
        wk_ref, wv_ref, ckv_ref, kr_ref, q_ref, k_ref, v_ref = rest
    else:
        ckv_ref, kr_ref, q_ref = rest
    tb, ts, d = x_ref.shape
    rows = tb * ts
    h = x_ref[...] * (1.0 + sc_ref[...]) + sh_ref[...]
    h = h.reshape(rows, d).astype(BF16)
    proj = _dot(h, win_ref[...])
    r0, r1 = MLA_Q_RANK, MLA_Q_RANK + MLA_KV_RANK

    def rms(z, g):
        return z * lax.rsqrt(jnp.mean(z * z, -1, keepdims=True) + LN_EPS) * g

    cq = rms(proj[:, :r0], qn_ref[...]).astype(BF16)
    ckv = rms(proj[:, r0:r1], kvn_ref[...])
    kra = proj[:, r1:r1 + LANES].reshape(tb, ts, LANES)
    krb = proj[:, r1 + LANES:r1 + 2 * LANES].reshape(tb, ts, LANES)
    kr = (kra * ck_ref[...] + krb * sk_ref[...]).reshape(rows, LANES)
    ckv_ref[...] = ckv
    kr_ref[...] = kr[:, :MLA_ROPE]
    nq = q_ref.shape[1]
    cqt = jnp.concatenate([cq_ref[...]] * MLA_HEADS, -1)
    sqt = jnp.concatenate([sq_ref[...]] * MLA_HEADS, -1)
    qa = _dot(cq, wqa_ref[...]).reshape(tb, ts, nq)
    qb = _dot(cq, wqb_ref[...]).reshape(tb, ts, nq)
    q_ref[...] = ((qa * cqt + qb * sqt) * qmul).reshape(rows, nq).astype(q_ref.dtype)
    if with_kv:
        kcat = jnp.concatenate([ckv.astype(BF16), kr.astype(BF16)], 1)
        k_ref[...] = _dot(kcat, wk_ref[...]).astype(k_ref.dtype)
        v_ref[...] = _dot_nt(wv_ref[...], ckv.astype(BF16)).astype(v_ref.dtype)


def _mla_weights(w_in, w_uq, w_uk, w_uv):
    d = w_in.shape[0]
    r0, r1 = MLA_Q_RANK, MLA_Q_RANK + MLA_KV_RANK
    half = MLA_ROPE // 2
    zpad = jnp.zeros((d, LANES - MLA_ROPE), F32)
    kr_w = w_in[:, r1:]
    kr_sw = jnp.concatenate([kr_w[:, half:], kr_w[:, :half]], 1)
    win = jnp.concatenate([w_in[:, :r1], kr_w, zpad, kr_sw, zpad], 1).astype(BF16)
    hd = MLA_NOPE + MLA_ROPE
    wq = w_uq.reshape(MLA_Q_RANK, MLA_HEADS, hd)
    z32 = jnp.zeros((MLA_Q_RANK, MLA_HEADS, LANES - hd), F32)
    z64 = jnp.zeros((MLA_Q_RANK, MLA_HEADS, MLA_NOPE), F32)
    wqa = jnp.concatenate([wq, z32], -1).reshape(MLA_Q_RANK, MLA_HEADS * LANES).astype(BF16)
    wqb = jnp.concatenate([z64, wq[..., MLA_NOPE + half:], wq[..., MLA_NOPE:MLA_NOPE + half], z32], -1)
    wqb = wqb.reshape(MLA_Q_RANK, MLA_HEADS * LANES).astype(BF16)
    top = jnp.concatenate([w_uk, jnp.zeros((MLA_KV_RANK, MLA_HEADS, LANES - MLA_NOPE), F32)], -1)
    eye = jnp.eye(MLA_ROPE, dtype=F32)[:, None, :]
    mid = jnp.concatenate([jnp.zeros((MLA_ROPE, MLA_HEADS, MLA_NOPE), F32),
                           jnp.broadcast_to(eye, (MLA_ROPE, MLA_HEADS, MLA_ROPE)),
                           jnp.zeros((MLA_ROPE, MLA_HEADS, LANES - hd), F32)], -1)
    bot = jnp.zeros((LANES - MLA_ROPE, MLA_HEADS, LANES), F32)
    wk = jnp.concatenate([top, mid, bot], 0).reshape(MLA_KV_RANK + LANES, MLA_HEADS * LANES).astype(BF16)
    wv = w_uv.reshape(MLA_KV_RANK, MLA_HEADS * MLA_V).astype(BF16)
    a_top = jnp.concatenate([w_uk.transpose(1, 2, 0),
                             jnp.zeros((MLA_HEADS, MLA_NOPE, LANES), F32)], -1)
    a_mid = jnp.concatenate([jnp.zeros((MLA_ROPE, MLA_KV_RANK), F32), jnp.eye(MLA_ROPE, dtype=F32),
                             jnp.zeros((MLA_ROPE, LANES - MLA_ROPE), F32)], -1)
    a_mid = jnp.broadcast_to(a_mid[None], (MLA_HEADS, MLA_ROPE, MLA_KV_RANK + LANES))
    a_bot = jnp.zeros((MLA_HEADS, LANES - hd, MLA_KV_RANK + LANES), F32)
    wabs = jnp.concatenate([a_top, a_mid, a_bot], 1).reshape(MLA_HEADS * LANES, MLA_KV_RANK + LANES).astype(BF16)
    return win, wqa, wqb, wk, wv, wabs


def _mla_proj(x, shift, scale, pos, win, qn, kvn, wqa, wqb, wk, wv, with_kv, qmul):
    b, s, d = x.shape
    tb, ts = _row_blocks(b, s)
    rows = tb * ts
    cq, sq, ck, sk = _rope_tables(pos)
    nq = MLA_HEADS * LANES
    in_specs = [_x_spec(tb, ts, s, d), _mod_spec(tb, ts, s, d), _mod_spec(tb, ts, s, d),
                _pos_spec(ts, s, LANES), _pos_spec(ts, s, LANES), _pos_spec(ts, s, LANES), _pos_spec(ts, s, LANES),
                _resident(win.shape), _resident((1, MLA_Q_RANK)), _resident((1, MLA_KV_RANK)),
                _resident(wqa.shape), _resident(wqb.shape)]
    args = [x, shift, scale, cq, sq, ck, sk, win, qn.reshape(1, -1), kvn.reshape(1, -1), wqa, wqb]
    out_specs = [_rows_spec(rows, MLA_KV_RANK), _rows_spec(rows, MLA_ROPE), _rows_spec(rows, nq)]
    out_shape = [jax.ShapeDtypeStruct((b * s, MLA_KV_RANK), F32), jax.ShapeDtypeStruct((b * s, MLA_ROPE), F32),
                 jax.ShapeDtypeStruct((b * s, nq), BF16 if with_kv else F32)]
    if with_kv:
        assert tb == 1
        nst = s // ts
        nv = MLA_HEADS * MLA_V
        wv_t = wv.T
        in_specs += [_resident(wk.shape), _resident(wv_t.shape)]
        args += [wk, wv_t]
        out_specs += [_rows_spec(rows, nq), pl.BlockSpec((None, nv, ts), lambda i: (i // nst, 0, i % nst))]
        out_shape += [jax.ShapeDtypeStruct((b * s, nq), BF16), jax.ShapeDtypeStruct((b, nv, s), BF16)]
    return pl.pallas_call(
        functools.partial(_mla_proj_kernel, with_kv=with_kv, qmul=qmul),
        grid=(b * s // rows,),
        in_specs=in_specs, out_specs=out_specs, out_shape=out_shape,
        compiler_params=_params("arbitrary"),
        name="mla_proj",
    )(*args)


def _mla_prompt_kernel(qi_ref, ki_ref, q_ref, k_ref, vt_ref, o_ref, m_sc, l_sc, acc_sc):
    pair = pl.program_id(2)
    qi, ki = qi_ref[pair], ki_ref[pair]
    t = q_ref.shape[0]

    @pl.when(ki == 0)
    def _():
        _init_stats(m_sc, l_sc, acc_sc)

    def update(diagonal):
        v_t = vt_ref[...]
        scores = [_dot_nt(k_ref[:, a * LANES:(a + 1) * LANES], q_ref[:, a * LANES:(a + 1) * LANES]) for a in range(2)]
        for a in range(2):
            s = scores[a]
            if diagonal:
                s = jnp.where(_lane((t, t), 0) <= _lane((t, t), 1), s, NEG_INF)
            _softmax_step([s], [v_t], m_sc, l_sc, acc_sc, a)

    pl.when(ki < qi)(functools.partial(update, False))
    pl.when(ki == qi)(functools.partial(update, True))

    @pl.when(ki == qi)
    def _():
        first = _lane((LANES, t), 0) < MLA_V
        o = jnp.where(first, acc_sc[0] * (1.0 / l_sc[0]), acc_sc[1] * (1.0 / l_sc[1]))
        o_ref[...] = o.T.astype(o_ref.dtype)


def _mla_prompt_attn(q2d, k2d, v_t, b, s):
    t = min(MLA_TILE, s)
    nt = s // t
    hp = MLA_HEADS // 2
    qis, kis = _causal_pairs(nt)
    grid_spec = pltpu.PrefetchScalarGridSpec(
        num_scalar_prefetch=2,
        grid=(b, hp, qis.shape[0]),
        in_specs=[pl.BlockSpec((None, t, 2 * LANES), lambda bb, h, p, qa, ka: (bb, qa[p], h)),
                  pl.BlockSpec((None, t, 2 * LANES), lambda bb, h, p, qa, ka: (bb, ka[p], h)),
                  pl.BlockSpec((None, LANES, t), lambda bb, h, p, qa, ka: (bb, h, ka[p]))],
        out_specs=pl.BlockSpec((None, t, LANES), lambda bb, h, p, qa, ka: (bb, qa[p], h)),
        scratch_shapes=[pltpu.VMEM((2, 1, t), F32), pltpu.VMEM((2, 1, t), F32), pltpu.VMEM((2, LANES, t), F32)])
    return pl.pallas_call(
        _mla_prompt_kernel,
        grid_spec=grid_spec,
        out_shape=jax.ShapeDtypeStruct((b, s, MLA_HEADS * MLA_V), BF16),
        compiler_params=_params("arbitrary", "arbitrary", "arbitrary"),
        name="mla_prompt_attn",
    )(qis, kis, q2d.reshape(b, s, -1), k2d.reshape(b, s, -1), v_t).reshape(b * s, -1)


def _mla_sample_kernel(pt_ref, q_ref, cn_ref, rn_ref, wabs_ref, wvt_ref, *rest, npg, qmul):
    c_refs, r_refs = rest[:npg], rest[npg:2 * npg]
    o_ref, qabs, rpad, eye, m_sc, l_sc, acc_sc = rest[2 * npg:]
    c = pl.program_id(1)
    nc = pl.num_programs(1)
    t = q_ref.shape[0]
    cols = MLA_HEADS * t
    page = c_refs[0].shape[0]

    @pl.when(c == 0)
    def _():
        q = q_ref[...]
        qs = jnp.concatenate([q[:, h * LANES:(h + 1) * LANES] for h in range(MLA_HEADS)], 0)
        rh = _lane((cols, LANES), 0) // t
        qbd = jnp.concatenate([jnp.where(rh == h, qs, 0.0) for h in range(MLA_HEADS)], 1).astype(BF16)
        qabs[...] = (_dot(qbd, wabs_ref[...]) * qmul).astype(BF16)
        rpad[...] = jnp.zeros(rpad.shape, F32)
        eye[...] = (_lane(eye.shape, 0) == _lane(eye.shape, 1)).astype(BF16)
        _init_stats(m_sc, l_sc, acc_sc)

    def update(lats, valid):
        scores, lat_ts = [], []
        for lat, j in lats:
            latb = lat.astype(BF16)
            s = _dot_nt(jnp.concatenate([latb, rpad[j].astype(BF16)], 1), qabs[...])
            scores.append(s if valid is None else jnp.where(valid, s, NEG_INF))
            lat_ts.append(_dot_nt(eye[...], latb).astype(BF16))
        _softmax_step(scores, lat_ts, m_sc, l_sc, acc_sc, 0)

    for i in range(npg):
        rpad[i, :, :MLA_ROPE] = r_refs[i][...]
    update([(c_refs[i][...], i) for i in range(npg)], None)

    @pl.when(c == nc - 1)
    def _():
        rpad[0] = jnp.zeros(rpad.shape[1:], F32)
        rpad[0, :t, :MLA_ROPE] = rn_ref[...]
        lat = jnp.concatenate([cn_ref[...], jnp.zeros((page - t, MLA_KV_RANK), F32)], 0)
        update([(lat, 0)], _lane((page, cols), 0) <= (_lane((page, cols), 1) % t))
        o_lat = (acc_sc[0] * (1.0 / l_sc[0])).astype(BF16)
        of = _dot(wvt_ref[...], o_lat)
        n = of.shape[0]
        own = (_lane((n, cols), 0) // MLA_V) == (_lane((n, cols), 1) // t)
        pick = ((_lane((t, cols), 1) % t) == _lane((t, cols), 0)).astype(BF16)
        o_ref[...] = _dot_nt(pick, jnp.where(own, of, 0.0).astype(BF16))


def _mla_sample_attn(q2d, ckv_new, kr_new, cache_ckv, cache_krope, slot, page_table, wabs, wv, db, t):
    page = cache_ckv.shape[2]
    n_pages = page_table.shape[1]
    npg = min(PAGES_PER_STEP, n_pages)
    assert n_pages % npg == 0
    cols = MLA_HEADS * t
    nv = MLA_HEADS * MLA_V
    wv_t = wv.T

    def page_spec(w, i):
        return pl.BlockSpec((None, None, page, w), lambda b, c, pt: (slot, pt[b, c * npg + i], 0, 0))

    grid_spec = pltpu.PrefetchScalarGridSpec(
        num_scalar_prefetch=1,
        grid=(db, n_pages // npg),
        in_specs=[pl.BlockSpec((None, t, MLA_HEADS * LANES), lambda b, c, pt: (b, 0, 0)),
                  pl.BlockSpec((None, t, MLA_KV_RANK), lambda b, c, pt: (b, 0, 0)),
                  pl.BlockSpec((None, t, MLA_ROPE), lambda b, c, pt: (b, 0, 0)),
                  pl.BlockSpec(wabs.shape, lambda b, c, pt: (0, 0)),
                  pl.BlockSpec(wv_t.shape, lambda b, c, pt: (0, 0))]
        + [page_spec(MLA_KV_RANK, i) for i in range(npg)] + [page_spec(MLA_ROPE, i) for i in range(npg)],
        out_specs=pl.BlockSpec((None, t, nv), lambda b, c, pt: (b, 0, 0)),
        scratch_shapes=[pltpu.VMEM((cols, MLA_KV_RANK + LANES), BF16), pltpu.VMEM((npg, page, LANES), F32),
                        pltpu.VMEM((MLA_KV_RANK, MLA_KV_RANK), BF16),
                        pltpu.VMEM((1, 1, cols), F32), pltpu.VMEM((1, 1, cols), F32),
                        pltpu.VMEM((1, MLA_KV_RANK, cols), F32)])
    o = pl.pallas_call(
        functools.partial(_mla_sample_kernel, npg=npg, qmul=(MLA_NOPE + MLA_ROPE) ** -0.5 * LOG2E),
        grid_spec=grid_spec,
        out_shape=jax.ShapeDtypeStruct((db, t, nv), F32),
        compiler_params=_params("arbitrary", "arbitrary"),
        name="mla_sample_attn",
    )(page_table, q2d.reshape(db, t, -1), ckv_new.reshape(db, t, -1), kr_new.reshape(db, t, -1), wabs, wv_t,
      *([cache_ckv] * npg), *([cache_krope] * npg))
    return o.reshape(db * t, nv)


def _router_kernel(x_ref, sh_ref, sc_ref, whi_ref, wlo_ref, b_ref, h_ref, lg_ref):
    tb, ts, d = x_ref.shape
    h = (x_ref[...] * (1.0 + sc_ref[...]) + sh_ref[...]).reshape(tb * ts, d)
    hi = h.astype(BF16)
    lo = (h - hi.astype(F32)).astype(BF16)
    h_ref[...] = hi
    lg = _dot(hi, whi_ref[...]) + _dot(hi, wlo_ref[...]) + _dot(lo, whi_ref[...]) + b_ref[...]
    lane = _lane(lg.shape, 1)
    far = jnp.int32(LANES)

    def top(vals):
        v = jnp.max(vals, -1, keepdims=True)
        return v, jnp.min(jnp.where(vals == v, lane, far), -1, keepdims=True)

    is_group = lane < MOE_GROUPS
    gmax, gsel = top(jnp.where(is_group, lg, NEG_INF))
    gprob = 1.0 / jnp.sum(jnp.where(is_group, jnp.exp(lg - gmax), 0.0), -1, keepdims=True)
    lo_lane = MOE_GROUPS + MOE_EXPERTS_PER_GROUP * gsel
    el = jnp.where((lane >= lo_lane) & (lane < lo_lane + MOE_EXPERTS_PER_GROUP), lg, NEG_INF)
    v1, i1 = top(el)
    v2, i2 = top(jnp.where(lane == i1, NEG_INF, el))
    ex = jnp.exp(v2 - v1)
    g1 = gprob / (1.0 + ex)
    g2 = g1 * ex
    e1 = (i1 - MOE_GROUPS).astype(F32)
    e2 = (i2 - MOE_GROUPS).astype(F32)
    lg_ref[...] = jnp.where(lane == 0, e1, jnp.where(lane == 1, e2, jnp.where(lane == 2, g1, jnp.where(lane == 3, g2, 0.0))))


def _router(x, shift, scale, whi, wlo, bias):
    b, s, d = x.shape
    tb, ts = _row_blocks(b, s)
    rows = tb * ts
    return pl.pallas_call(
        _router_kernel,
        grid=(b * s // rows,),
        in_specs=[_x_spec(tb, ts, s, d), _mod_spec(tb, ts, s, d), _mod_spec(tb, ts, s, d),
                  _resident(whi.shape), _resident(wlo.shape), _resident(bias.shape)],
        out_specs=[_rows_spec(rows, d), _rows_spec(rows, LANES)],
        out_shape=[jax.ShapeDtypeStruct((b * s, d), BF16), jax.ShapeDtypeStruct((b * s, LANES), F32)],
        compiler_params=_params("arbitrary"),
        name="router",
    )(x, shift, scale, whi, wlo, bias)


def _expert_kernel(blk_ref, exp_ref, first_ref, fresh_ref, lo_ref, hi_ref, n_ref, x_ref, w1_ref, w3_ref, w2_ref,
                   y_ref, w1_sc, w3_sc, w2_sc):
    w = pl.program_id(0)

    @pl.when(w < n_ref[0])
    def _():
        @pl.when(fresh_ref[w] == 1)
        def _():
            w1_sc[...] = w1_ref[...].astype(BF16)
            w3_sc[...] = w3_ref[...].astype(BF16)
            w2_sc[...] = w2_ref[...].astype(BF16)

        x = x_ref[...]
        a = _dot(x, w1_sc[...])
        u = (a * jax.nn.sigmoid(a)) * _dot(x, w3_sc[...])
        y = _dot(u.astype(BF16), w2_sc[...]).astype(y_ref.dtype)
        row = blk_ref[w] * x.shape[0] + _lane((x.shape[0], 1), 0)
        mine = (row >= lo_ref[w]) & (row < hi_ref[w])

        @pl.when(first_ref[w] == 1)
        def _():
            y_ref[...] = jnp.where(mine, y, jnp.zeros_like(y))

        @pl.when(first_ref[w] == 0)
        def _():
            y_ref[...] = jnp.where(mine, y, y_ref[...])


def _experts(xs, items, w1, w3, w2):
    a, d = xs.shape
    ff = w1.shape[2]
    n_items = items[0].shape[0]
    grid_spec = pltpu.PrefetchScalarGridSpec(
        num_scalar_prefetch=7,
        grid=(n_items,),
        in_specs=[pl.BlockSpec((MOE_ROWS, d), lambda w, blk, ex, *_: (blk[w], 0)),
                  pl.BlockSpec((None, d, ff), lambda w, blk, ex, *_: (ex[w], 0, 0)),
                  pl.BlockSpec((None, d, ff), lambda w, blk, ex, *_: (ex[w], 0, 0)),
                  pl.BlockSpec((None, ff, d), lambda w, blk, ex, *_: (ex[w], 0, 0))],
        out_specs=pl.BlockSpec((MOE_ROWS, d), lambda w, blk, ex, *_: (blk[w], 0)),
        scratch_shapes=[pltpu.VMEM((d, ff), BF16), pltpu.VMEM((d, ff), BF16), pltpu.VMEM((ff, d), BF16)])
    return pl.pallas_call(
        _expert_kernel,
        grid_spec=grid_spec,
        out_shape=jax.ShapeDtypeStruct((a, d), BF16),
        compiler_params=_params("arbitrary"),
        name="experts",
    )(*items, xs, w1, w3, w2)


def _combine_ln_kernel(y0_ref, y1_ref, r_ref, x_ref, gate_ref, g_ref, b_ref, o_ref, *, alpha):
    tb, ts, d = x_ref.shape
    r = r_ref[...]
    y = y0_ref[...].astype(F32) * r[:, 2:3] + y1_ref[...].astype(F32) * r[:, 3:4]
    z = alpha * x_ref[...] + gate_ref[...] * y.reshape(tb, ts, d)
    o_ref[...] = _layernorm(z, g_ref[...], b_ref[...])


def _combine_ln(y0, y1, route, row_off, x, gate, g, b, alpha):
    bb, s, d = x.shape
    tb, ts = _row_blocks(bb, s)
    rows = tb * ts
    assert row_off % rows == 0
    off = row_off // rows
    return pl.pallas_call(
        functools.partial(_combine_ln_kernel, alpha=alpha),
        grid=(bb * s // rows,),
        in_specs=[_rows_spec(rows, d, off), _rows_spec(rows, d, off), _rows_spec(rows, LANES, off),
                  _x_spec(tb, ts, s, d), _mod_spec(tb, ts, s, d), _resident((1, 1, d)), _resident((1, 1, d))],
        out_specs=_x_spec(tb, ts, s, d),
        out_shape=jax.ShapeDtypeStruct(x.shape, F32),
        compiler_params=_params("arbitrary"),
        name="combine_ln",
    )(y0, y1, route, x, gate, g.reshape(1, 1, d), b.reshape(1, 1, d))


def _dispatch(eid):
    n = eid.shape[0]
    a = n * MOE_TOPK
    n_blocks = -(-a // MOE_ROWS)
    flat_e = eid.reshape(a)
    order = jnp.argsort(flat_e).astype(jnp.int32)
    inv = jnp.argsort(order).astype(jnp.int32).reshape(n, MOE_TOPK)
    counts = jnp.sum((flat_e[:, None] == jnp.arange(MOE_EXPERTS)[None, :]).astype(jnp.int32), 0)
    end = jnp.cumsum(counts)
    start = end - counts
    b0 = start // MOE_ROWS
    nb = jnp.where(counts > 0, (end - 1) // MOE_ROWS - b0 + 1, 0)
    cum = jnp.cumsum(nb)
    n_items = n_blocks + MOE_EXPERTS - 1
    w = jnp.arange(n_items)
    ex = jnp.minimum(jnp.sum((cum[None, :] <= w[:, None]).astype(jnp.int32), 1), MOE_EXPERTS - 1)
    live = w < cum[-1]
    blk = jnp.where(live, b0[ex] + w - (cum - nb)[ex], n_blocks - 1)
    ex = jnp.where(live, ex, ex[jnp.maximum(cum[-1] - 1, 0)])
    prev_blk = jnp.concatenate([jnp.full((1,), -1, blk.dtype), blk[:-1]])
    prev_ex = jnp.concatenate([jnp.full((1,), -1, ex.dtype), ex[:-1]])
    items = [blk, ex, blk != prev_blk, ex != prev_ex, start[ex], end[ex], cum[-1:]]
    return order, inv, [x.astype(jnp.int32) for x in items]


def _moe_layer(yp, ys, mod_p, mod_s, g, b, wg, bg, we, be, w1, w3, w2, alpha):
    d = yp.shape[-1]
    npad = LANES - MOE_GROUPS - MOE_EXPERTS
    wr = jnp.concatenate([wg, we, jnp.zeros((d, npad), F32)], 1)
    whi = wr.astype(BF16)
    wlo = (wr - whi.astype(F32)).astype(BF16)
    br = jnp.concatenate([bg, be, jnp.zeros((npad,), F32)]).reshape(1, LANES)
    hp, rp = _router(yp, mod_p[3], mod_p[4], whi, wlo, br)
    hs, rs = _router(ys, mod_s[3], mod_s[4], whi, wlo, br)
    h = jnp.concatenate([hp, hs], 0)
    route = jnp.concatenate([rp, rs], 0)
    order, inv, items = _dispatch(route[:, :MOE_TOPK].astype(jnp.int32))
    yb = _experts(h[order // MOE_TOPK], items, w1, w3, w2)
    y0, y1 = yb[inv[:, 0]], yb[inv[:, 1]]
    yp = _combine_ln(y0, y1, route, 0, yp, mod_p[5], g, b, alpha)
    ys = _combine_ln(y0, y1, route, hp.shape[0], ys, mod_s[5], g, b, alpha)
    return yp, ys


def kernel(x_prompt, x_sample, cache_diff_k, cache_diff_v, cache_dil_kv_g0, cache_dil_kv_g1, cache_dil_kv_g2, cache_mla_ckv, cache_mla_krope, page_table, c_prompt, c_sample, rel_bias, ada_w, ada_b, ln_g, ln_b, diff_w_in, diff_lam, diff_subln, diff_w_out, dil_w_in, dil_w_out, mla_w_in, mla_q_norm, mla_kv_norm, mla_w_uq, mla_w_uk, mla_w_uv, mla_w_out, moe_wg, moe_bg, moe_we, moe_be, moe_w1, moe_w3, moe_w2):
    depth = ada_w.shape[0]
    b, s, d = x_prompt.shape
    db, t, _ = x_sample.shape
    past = page_table.shape[1] * cache_diff_k.shape[2]
    alpha = (2 * depth) ** 0.25
    dil_cache = (cache_dil_kv_g0, cache_dil_kv_g1, cache_dil_kv_g2)
    bias_d = _bias_by_dist(rel_bias, max(s, past + t))

    mods = _ada(jnp.concatenate([c_prompt, c_sample], 0), ada_w, ada_b)
    yp, ys = x_prompt, x_sample
    dk_p, dv_p, dk_s, dv_s = [], [], [], []
    dl_p = [[] for _ in DIL_PATTERNS]
    dl_s = [[] for _ in DIL_PATTERNS]
    ck_p, kr_p, ck_s, kr_s = [], [], [], []
    for i in range(depth):
        kind, j = i % N_MIXERS, i // N_MIXERS
        mod_p = [mods[i, :b, None, m * d:(m + 1) * d] for m in range(6)]
        mod_s = [mods[i, b:, None, m * d:(m + 1) * d] for m in range(6)]
        if kind == 0:
            nq = DIFF_HEADS * 2 * DIFF_DH
            nk = DIFF_KV_HEADS * 2 * DIFF_DH
            qkv = ((0, nq, 1.0), (nq, nq + nk, 1.0), (nq + nk, nq + 2 * nk, 1.0))
            splits_p = ((0, nq, DIFF_DH ** -0.5 * LOG2E),) + qkv[1:] + ((nq, nq + nk, 1.0),)
            w_in = diff_w_in[j].astype(BF16)
            lam_init = 0.8 - 0.6 * math.exp(-0.3 * i)
            lp = diff_lam[j].astype(F32)
            lam = jnp.exp(jnp.sum(lp[0] * lp[1])) - jnp.exp(jnp.sum(lp[2] * lp[3])) + lam_init
            qp, kp, vp, kbp, vtp = _proj(yp, mod_p[0], mod_p[1], w_in, splits_p, (BF16, F32, F32, BF16),
                                         wt=w_in[:, nq + nk:].T)
            qs, ks, vs = _proj(ys, mod_s[0], mod_s[1], w_in, qkv, (F32, F32, F32))
            o_p = _diff_prompt_attn(qp, kbp, vtp, bias_d, lam, lam_init, diff_subln[j], b, s)
            o_s = _diff_sample_attn(qs, ks, vs, cache_diff_k, cache_diff_v, j, page_table, bias_d, lam, lam_init,
                                    diff_subln[j], db, t)
            dk_p.append(kp.reshape(b, s, DIFF_KV_HEADS, 2 * DIFF_DH))
            dv_p.append(vp.reshape(b, s, DIFF_KV_HEADS, 2 * DIFF_DH))
            dk_s.append(ks.reshape(db, t, DIFF_KV_HEADS, 2 * DIFF_DH))
            dv_s.append(vs.reshape(db, t, DIFF_KV_HEADS, 2 * DIFF_DH))
            w_out = diff_w_out[j].astype(BF16)
            yp = _outproj_ln(o_p, yp, mod_p[2], w_out, ln_g[i, 0], ln_b[i, 0], alpha)
            ys = _outproj_ln(o_s, ys, mod_s[2], w_out, ln_g[i, 0], ln_b[i, 0], alpha)
        elif kind == 1:
            hw = DIL_HEADS * DIL_DH
            splits, dts_p, dts_s = [], [], []
            for g in range(len(DIL_PATTERNS)):
                splits += [(3 * g * hw, (3 * g + 1) * hw, 1.0), ((3 * g + 1) * hw, (3 * g + 3) * hw, 1.0)]
                dts_p += [BF16, F32]
                dts_s += [F32, F32]
            w_in = dil_w_in[j].astype(BF16)
            pp = _proj(yp, mod_p[0], mod_p[1], w_in, splits, dts_p)
            ps = _proj(ys, mod_s[0], mod_s[1], w_in, splits, dts_s)
            outs, lses = [], []
            for g, (win, dil) in enumerate(DIL_PATTERNS):
                o_g, lse_g = _dil_prompt_group(pp[2 * g], pp[2 * g + 1], bias_d, win, dil, b, s)
                outs.append(o_g)
                lses.append(lse_g)
                kv = pp[2 * g + 1].reshape(b, s, 2, DIL_HEADS, DIL_DH)
                dl_p[g].append(kv[:, s - min(win, s):])
            bufs = [cb[j] for cb in dil_cache]
            kv_news = [ps[2 * g + 1] for g in range(len(DIL_PATTERNS))]
            o_s = _dil_sample_attn([ps[2 * g] for g in range(len(DIL_PATTERNS))], kv_news, bufs, bias_d, past, db, t)
            for g in range(len(DIL_PATTERNS)):
                new = kv_news[g].reshape(db, t, 2, DIL_HEADS, DIL_DH).astype(bufs[g].dtype)
                dl_s[g].append(jnp.concatenate([bufs[g], new], 1)[:, -bufs[g].shape[1]:])
            w_out = dil_w_out[j].astype(BF16)
            yp = _merge_outproj_ln(outs, lses, yp, mod_p[2], w_out, ln_g[i, 0], ln_b[i, 0], alpha)
            ys = _outproj_ln(o_s, ys, mod_s[2], w_out, ln_g[i, 0], ln_b[i, 0], alpha)
        else:
            win, wqa, wqb, wk, wv, wabs = _mla_weights(mla_w_in[j], mla_w_uq[j], mla_w_uk[j], mla_w_uv[j])
            ckv_p, krp, qp, kfp, vtp = _mla_proj(yp, mod_p[0], mod_p[1], jnp.arange(s), win, mla_q_norm[j],
                                                 mla_kv_norm[j], wqa, wqb, wk, wv, True,
                                                 (MLA_NOPE + MLA_ROPE) ** -0.5 * LOG2E)
            ckv_s, krs, qs = _mla_proj(ys, mod_s[0], mod_s[1], past + jnp.arange(t), win, mla_q_norm[j],
                                       mla_kv_norm[j], wqa, wqb, wk, wv, False, 1.0)
            o_p = _mla_prompt_attn(qp, kfp, vtp, b, s)
            o_s = _mla_sample_attn(qs, ckv_s, krs, cache_mla_ckv, cache_mla_krope, j, page_table, wabs, wv, db, t)
            ck_p.append(ckv_p.reshape(b, s, MLA_KV_RANK))
            kr_p.append(krp.reshape(b, s, MLA_ROPE))
            ck_s.append(ckv_s.reshape(db, t, MLA_KV_RANK))
            kr_s.append(krs.reshape(db, t, MLA_ROPE))
            w_out = mla_w_out[j].astype(BF16)
            yp = _outproj_ln(o_p, yp, mod_p[2], w_out, ln_g[i, 0], ln_b[i, 0], alpha)
            ys = _outproj_ln(o_s, ys, mod_s[2], w_out, ln_g[i, 0], ln_b[i, 0], alpha)
        yp, ys = _moe_layer(yp, ys, mod_p, mod_s, ln_g[i, 1], ln_b[i, 1], moe_wg[i], moe_bg[i], moe_we[i],
                            moe_be[i], moe_w1[i], moe_w3[i], moe_w2[i], alpha)
    return (yp, ys, jnp.stack(dk_p), jnp.stack(dv_p), jnp.stack(dk_s), jnp.stack(dv_s),
            jnp.stack(dl_p[0]), jnp.stack(dl_p[1]), jnp.stack(dl_p[2]),
            jnp.stack(dl_s[0]), jnp.stack(dl_s[1]), jnp.stack(dl_s[2]),
            jnp.stack(ck_p), jnp.stack(kr_p), jnp.stack(ck_s), jnp.stack(kr_s))
```

```python
import functools
import math

import jax
import jax.numpy as jnp
from jax import lax
from jax.experimental import pallas as pl
from jax.experimental.pallas import tpu as pltpu

F32 = jnp.float32
BF16 = jnp.bfloat16

N_MIXERS = 3
N_BUCKETS = 32
REL_MAX_DIST = 2048
DIFF_DH = 64
DIFF_HEADS = 8
DIFF_KV_HEADS = 2
DIL_PATTERNS = ((128, 1), (512, 4), (2048, 16))
DIL_HEADS = 8
DIL_DH = 64
MLA_HEADS = 16
MLA_NOPE = 64
MLA_ROPE = 32
MLA_V = 64
MLA_Q_RANK = 384
MLA_KV_RANK = 256
ROPE_BASE = 10000.0
MOE_GROUPS = 4
MOE_EXPERTS_PER_GROUP = 8
MOE_EXPERTS = MOE_GROUPS * MOE_EXPERTS_PER_GROUP
MOE_TOPK = 2
LN_EPS = 1e-5
NEG_INF = -1e30
LOG2E = math.log2(math.e)

LANES = 128
ROW_TILE = 512
MOE_ROWS = 256
DIFF_TILE = 256
MLA_TILE = 512
MLA_HEADS_PER_STEP = 4
DIL_TILE = 128
PAGES_PER_STEP = 32
VMEM_LIMIT_BYTES = 56 * 1024 * 1024


def _params(*sem):
    return pltpu.CompilerParams(dimension_semantics=sem, vmem_limit_bytes=VMEM_LIMIT_BYTES)


def _dot(a, b):
    return jnp.dot(a, b, preferred_element_type=F32)


def _dot_nt(a, b):
    return lax.dot_general(a, b, (((1,), (1,)), ((), ())), preferred_element_type=F32)


def _resident(shape):
    nd = len(shape)
    return pl.BlockSpec(shape, lambda *_: (0,) * nd)


def _lane(shape, axis):
    return lax.broadcasted_iota(jnp.int32, shape, axis)


def _rel_bucket(dist):
    max_exact = N_BUCKETS // 2
    n = jnp.maximum(dist, 0)
    nf = jnp.maximum(n, 1).astype(F32)
    large = max_exact + (jnp.log(nf / max_exact) / math.log(REL_MAX_DIST / max_exact)
                         * (N_BUCKETS - max_exact)).astype(jnp.int32)
    return jnp.where(n < max_exact, n, jnp.minimum(large, N_BUCKETS - 1))


def _bias_by_dist(rel_bias, n):
    hit = _rel_bucket(jnp.arange(n))[None, :, None] == jnp.arange(N_BUCKETS)[None, None, :]
    return jnp.sum(jnp.where(hit, rel_bias.astype(F32).T[:, None, :], 0.0), -1)


def _toeplitz(w, rows, cols):
    p = rows + cols
    u = jnp.concatenate([w[..., :cols][..., ::-1], w[..., :1], w[..., cols:][..., ::-1]], -1)
    t = jnp.tile(u, (1,) * (w.ndim - 1) + (rows,))[..., :rows * (p - 1)]
    return t.reshape(w.shape[:-1] + (rows, p - 1))[..., :cols]


def _neg(shape):
    return jnp.full(shape, NEG_INF, F32)


def _rope_tables(pos):
    half = MLA_ROPE // 2
    inv = ROPE_BASE ** (-jnp.arange(half, dtype=F32) / half)
    ang = pos.astype(F32)[:, None] * inv[None, :]
    cos, sin = jnp.cos(ang), jnp.sin(ang)
    n = pos.shape[0]
    cq = jnp.concatenate([jnp.ones((n, MLA_NOPE), F32), cos, cos, jnp.zeros((n, 32), F32)], 1)
    sq = jnp.concatenate([jnp.zeros((n, MLA_NOPE), F32), -sin, sin, jnp.zeros((n, 32), F32)], 1)
    ck = jnp.concatenate([cos, cos, jnp.zeros((n, LANES - MLA_ROPE), F32)], 1)
    sk = jnp.concatenate([-sin, sin, jnp.zeros((n, LANES - MLA_ROPE), F32)], 1)
    return cq[None], sq[None], ck[None], sk[None]


def _ada_kernel(c_ref, w_ref, b_ref, o_ref):
    c = c_ref[...]
    a = (c * jax.nn.sigmoid(c)).astype(BF16)
    o_ref[...] = _dot(a, w_ref[...].astype(BF16)) + b_ref[...]


def _ada(c_all, ada_w, ada_b):
    depth, d, n = ada_w.shape
    m = c_all.shape[0]
    tn = 1536
    return pl.pallas_call(
        _ada_kernel,
        grid=(depth, n // tn),
        in_specs=[pl.BlockSpec((m, d), lambda l, j: (0, 0)),
                  pl.BlockSpec((None, d, tn), lambda l, j: (l, 0, j)),
                  pl.BlockSpec((None, 1, tn), lambda l, j: (l, 0, j))],
        out_specs=pl.BlockSpec((None, m, tn), lambda l, j: (l, 0, j)),
        out_shape=jax.ShapeDtypeStruct((depth, m, n), F32),
        compiler_params=_params("arbitrary", "arbitrary"),
        name="ada",
    )(c_all, ada_w, ada_b.reshape(depth, 1, n))


def _row_blocks(b, s):
    if s >= ROW_TILE:
        assert s % ROW_TILE == 0
        return 1, ROW_TILE
    tb = min(b, max(1, ROW_TILE // s))
    assert b % tb == 0
    return tb, s


def _x_spec(tb, ts, s, d):
    nst = s // ts
    return pl.BlockSpec((tb, ts, d), lambda i: (i // nst, i % nst, 0))


def _mod_spec(tb, ts, s, d):
    nst = s // ts
    return pl.BlockSpec((tb, 1, d), lambda i: (i // nst, 0, 0))


def _pos_spec(ts, s, c):
    nst = s // ts
    return pl.BlockSpec((1, ts, c), lambda i: (0, i % nst, 0))


def _rows_spec(rows, n, off=0):
    return pl.BlockSpec((rows, n), lambda i: (i + off, 0))


def _proj_kernel(x_ref, sh_ref, sc_ref, w_ref, *rest, splits, transposed):
    wt_ref = rest[0] if transposed else None
    o_refs = rest[1:] if transposed else rest
    tb, ts, d = x_ref.shape
    h = x_ref[...] * (1.0 + sc_ref[...]) + sh_ref[...]
    h = h.reshape(tb * ts, d).astype(BF16)
    for o_ref, (c0, c1, mul) in zip(o_refs, splits):
        for s in range(c0, c1, 512):
            e = min(s + 512, c1)
            o_ref[:, s - c0:e - c0] = (_dot(h, w_ref[:, s:e]) * mul).astype(o_ref.dtype)
    if transposed:
        o_refs[-1][...] = _dot_nt(wt_ref[...], h).astype(o_refs[-1].dtype)


def _proj(x, shift, scale, w, splits, dtypes, wt=None):
    b, s, d = x.shape
    tb, ts = _row_blocks(b, s)
    rows = tb * ts
    nst = s // ts
    in_specs = [_x_spec(tb, ts, s, d), _mod_spec(tb, ts, s, d), _mod_spec(tb, ts, s, d), _resident(w.shape)]
    out_specs = [_rows_spec(rows, c1 - c0) for (c0, c1, _) in splits]
    out_shape = [jax.ShapeDtypeStruct((b * s, c1 - c0), dt) for (c0, c1, _), dt in zip(splits, dtypes)]
    args = [x, shift, scale, w]
    if wt is not None:
        assert tb == 1
        in_specs.append(_resident(wt.shape))
        args.append(wt)
        out_specs.append(pl.BlockSpec((None, wt.shape[0], ts), lambda i: (i // nst, 0, i % nst)))
        out_shape.append(jax.ShapeDtypeStruct((b, wt.shape[0], s), BF16))
    return pl.pallas_call(
        functools.partial(_proj_kernel, splits=tuple(splits), transposed=wt is not None),
        grid=(b * s // rows,),
        in_specs=in_specs, out_specs=out_specs, out_shape=out_shape,
        compiler_params=_params("arbitrary"),
        name="proj",
    )(*args)


def _layernorm(z, g, b):
    mu = jnp.mean(z, -1, keepdims=True)
    zc = z - mu
    var = jnp.mean(zc * zc, -1, keepdims=True)
    return zc * lax.rsqrt(var + LN_EPS) * g + b


def _outproj_ln_kernel(o_ref, x_ref, gate_ref, w_ref, g_ref, b_ref, y_ref, *, alpha):
    tb, ts, d = x_ref.shape
    f = _dot(o_ref[...].astype(BF16), w_ref[...]).reshape(tb, ts, d)
    z = alpha * x_ref[...] + gate_ref[...] * f
    y_ref[...] = _layernorm(z, g_ref[...], b_ref[...])


def _outproj_ln(o2d, x, gate, w, g, b, alpha):
    bb, s, d = x.shape
    tb, ts = _row_blocks(bb, s)
    rows = tb * ts
    return pl.pallas_call(
        functools.partial(_outproj_ln_kernel, alpha=alpha),
        grid=(bb * s // rows,),
        in_specs=[_rows_spec(rows, o2d.shape[1]), _x_spec(tb, ts, s, d), _mod_spec(tb, ts, s, d),
                  _resident(w.shape), _resident((1, 1, d)), _resident((1, 1, d))],
        out_specs=_x_spec(tb, ts, s, d),
        out_shape=jax.ShapeDtypeStruct(x.shape, F32),
        compiler_params=_params("arbitrary"),
        name="outproj_ln",
    )(o2d, x, gate, w, g.reshape(1, 1, d), b.reshape(1, 1, d))


def _merge_outproj_ln_kernel(o0, o1, o2, l0, l1, l2, x_ref, gate_ref, w_ref, g_ref, b_ref, y_ref, *, alpha):
    tb, ts, d = x_ref.shape
    a0, a1, a2 = l0[...], l1[...], l2[...]
    mx = jnp.maximum(jnp.maximum(a0, a1), a2)
    e0, e1, e2 = jnp.exp(a0 - mx), jnp.exp(a1 - mx), jnp.exp(a2 - mx)
    o = (e0 * o0[...] + e1 * o1[...] + e2 * o2[...]) / (e0 + e1 + e2)
    f = _dot(o.astype(BF16), w_ref[...]).reshape(tb, ts, d)
    z = alpha * x_ref[...] + gate_ref[...] * f
    y_ref[...] = _layernorm(z, g_ref[...], b_ref[...])


def _merge_outproj_ln(outs, lses, x, gate, w, g, b, alpha):
    bb, s, d = x.shape
    tb, ts = _row_blocks(bb, s)
    rows = tb * ts
    k = outs[0].shape[1]
    return pl.pallas_call(
        functools.partial(_merge_outproj_ln_kernel, alpha=alpha),
        grid=(bb * s // rows,),
        in_specs=[_rows_spec(rows, k)] * 6 + [_x_spec(tb, ts, s, d), _mod_spec(tb, ts, s, d),
                                              _resident(w.shape), _resident((1, 1, d)), _resident((1, 1, d))],
        out_specs=_x_spec(tb, ts, s, d),
        out_shape=jax.ShapeDtypeStruct(x.shape, F32),
        compiler_params=_params("arbitrary"),
        name="merge_outproj_ln",
    )(*outs, *lses, x, gate, w, g.reshape(1, 1, d), b.reshape(1, 1, d))


def _softmax_step(s_blocks, vt_blocks, m_sc, l_sc, acc_sc, i):
    m_prev = m_sc[i]
    m_new = m_prev
    for s in s_blocks:
        m_new = jnp.maximum(m_new, jnp.max(s, 0, keepdims=True))
    a = jnp.exp2(m_prev - m_new)
    l = a * l_sc[i]
    acc = a * acc_sc[i]
    for s, v_t in zip(s_blocks, vt_blocks):
        p = jnp.exp2(s - m_new)
        l = l + jnp.sum(p, 0, keepdims=True)
        acc = acc + _dot(v_t, p.astype(BF16))
    l_sc[i] = l
    acc_sc[i] = acc
    m_sc[i] = m_new


def _causal_pairs(nt, kt=1):
    pairs = [(qi, ki) for qi in range(nt) for ki in range(qi // kt + 1)]
    return (jnp.asarray([p[0] for p in pairs], jnp.int32), jnp.asarray([p[1] for p in pairs], jnp.int32))


def _init_stats(m_sc, l_sc, acc_sc):
    m_sc[...] = jnp.full(m_sc.shape, NEG_INF, F32)
    l_sc[...] = jnp.zeros(l_sc.shape, F32)
    acc_sc[...] = jnp.zeros(acc_sc.shape, F32)


def _diff_prompt_kernel(qi_ref, ki_ref, lam_ref, q_ref, k_ref, vt_ref, bias_ref, subln_ref, o_ref, m_sc, l_sc, acc_sc,
                        *, post):
    pair = pl.program_id(2)
    qi, ki = qi_ref[pair], ki_ref[pair]
    r_heads = q_ref.shape[1] // LANES
    t = q_ref.shape[0]
    kt = k_ref.shape[0] // t

    @pl.when(ki == 0)
    def _():
        _init_stats(m_sc, l_sc, acc_sc)

    k = k_ref[...]
    v_ts = [vt_ref[:, j * t:(j + 1) * t] for j in range(kt)]
    first = _lane(k.shape, 1) < DIFF_DH
    rows = k.shape[0]
    k2 = jnp.concatenate([jnp.where(first, k, jnp.zeros_like(k)), jnp.where(first, jnp.zeros_like(k), k)], 0)
    scores = [_dot_nt(k2, q_ref[:, r * LANES:(r + 1) * LANES]) for r in range(r_heads)]
    for r in range(r_heads):
        biases = [bias_ref[r, jnp.maximum(qi - (ki * kt + j) + 1, 0)] for j in range(kt)]
        for mi in range(2):
            blocks = [scores[r][mi * rows + j * t:mi * rows + (j + 1) * t] + biases[j] for j in range(kt)]
            _softmax_step(blocks, v_ts, m_sc, l_sc, acc_sc, 2 * r + mi)

    @pl.when(ki == qi // kt)
    def _():
        lam = lam_ref[0, 0]
        for r in range(r_heads):
            o = acc_sc[2 * r] * (1.0 / l_sc[2 * r]) - lam * (acc_sc[2 * r + 1] * (1.0 / l_sc[2 * r + 1]))
            o = o * lax.rsqrt(jnp.mean(o * o, 0, keepdims=True) + LN_EPS) * subln_ref[...] * post
            o_ref[:, r * LANES:(r + 1) * LANES] = o.T.astype(o_ref.dtype)


def _diff_prompt_attn(q2d, kb2d, v_t, bias_d, lam, lam_init, subln, b, s):
    t = min(DIFF_TILE, s)
    nt = s // t
    r = DIFF_HEADS // DIFF_KV_HEADS
    gx = jnp.concatenate([_neg((DIFF_HEADS, t)), bias_d[:, :nt * t]], 1)
    win = jnp.stack([gx[:, d * t + 1:d * t + 2 * t] for d in range(nt)], 1)[..., ::-1]
    tiles = _toeplitz(win, t, t) * LOG2E
    tiles = jnp.concatenate([_neg((DIFF_HEADS, 1, t, t)), tiles], 1)
    hw = r * LANES
    kt = 2 if nt % 2 == 0 else 1
    qis, kis = _causal_pairs(nt, kt)
    grid_spec = pltpu.PrefetchScalarGridSpec(
        num_scalar_prefetch=2,
        grid=(DIFF_KV_HEADS, b, qis.shape[0]),
        in_specs=[pl.BlockSpec(memory_space=pltpu.SMEM),
                  pl.BlockSpec((None, t, hw), lambda g, bb, p, qa, ka: (bb, qa[p], g)),
                  pl.BlockSpec((None, kt * t, LANES), lambda g, bb, p, qa, ka: (bb, ka[p], g)),
                  pl.BlockSpec((None, LANES, kt * t), lambda g, bb, p, qa, ka: (bb, g, ka[p])),
                  pl.BlockSpec((r, nt + 1, t, t), lambda g, bb, p, qa, ka: (g, 0, 0, 0)),
                  pl.BlockSpec((LANES, 1), lambda g, bb, p, qa, ka: (0, 0))],
        out_specs=pl.BlockSpec((None, t, hw), lambda g, bb, p, qa, ka: (bb, qa[p], g)),
        scratch_shapes=[pltpu.VMEM((2 * r, 1, t), F32), pltpu.VMEM((2 * r, 1, t), F32),
                        pltpu.VMEM((2 * r, LANES, t), F32)])
    return pl.pallas_call(
        functools.partial(_diff_prompt_kernel, post=1.0 - lam_init),
        grid_spec=grid_spec,
        out_shape=jax.ShapeDtypeStruct((b, s, DIFF_HEADS * LANES), BF16),
        compiler_params=_params("arbitrary", "arbitrary", "arbitrary"),
        name="diff_prompt_attn",
    )(qis, kis, lam.reshape(1, 1), q2d.reshape(b, s, -1), kb2d.reshape(b, s, -1), v_t, tiles,
      subln.reshape(LANES, 1)).reshape(b * s, -1)


def _as_column(row):
    n = row.shape[1]
    return jnp.sum(jnp.where(_lane((n, n), 0) == _lane((n, n), 1), row, 0.0), 1, keepdims=True)


def _decode_update(scores, vals, m_sc, l_sc, acc_sc):
    m_prev = m_sc[...]
    m_new = m_prev
    for s in scores:
        m_new = jnp.maximum(m_new, jnp.max(s, 0, keepdims=True))
    a = jnp.exp2(m_prev - m_new)
    l = a * l_sc[...]
    pv = None
    for s, v in zip(scores, vals):
        p = jnp.exp2(s - m_new)
        l = l + jnp.sum(p, 0, keepdims=True)
        part = _dot(p.T.astype(BF16), v)
        pv = part if pv is None else pv + part
    acc_sc[...] = _as_column(a) * acc_sc[...] + pv
    l_sc[...] = l
    m_sc[...] = m_new


def _diff_sample_kernel(pt_ref, lam_ref, q_ref, kn_ref, vn_ref, bias_ref, subln_ref, *rest, npg, qmul, post):
    k_refs, v_refs = rest[:npg], rest[npg:2 * npg]
    o_ref, qbd, m_sc, l_sc, acc_sc = rest[2 * npg:]
    c = pl.program_id(1)
    nc = pl.num_programs(1)
    t = q_ref.shape[0]

    @pl.when(c == 0)
    def _():
        q = q_ref[...] * qmul
        first = _lane((t, LANES), 1) < DIFF_DH
        zero = jnp.zeros((t, LANES), F32)
        heads = [q[:, h * LANES:(h + 1) * LANES] for h in range(DIFF_HEADS)]
        pieces = [jnp.where(first, qh, zero) for qh in heads] + [jnp.where(first, zero, qh) for qh in heads]
        qbd[...] = jnp.concatenate(pieces, 0).astype(BF16)
        _init_stats(m_sc, l_sc, acc_sc)

    def update(pages):
        scores = [_dot_nt(kp.astype(BF16), qbd[...]) + bias for kp, _, bias in pages]
        _decode_update(scores, [vp.astype(BF16) for _, vp, _ in pages], m_sc, l_sc, acc_sc)

    update([(k_refs[i][...], v_refs[i][...], bias_ref[c * npg + i]) for i in range(npg)])

    @pl.when(c == nc - 1)
    def _():
        pad = jnp.zeros((k_refs[0].shape[0] - kn_ref.shape[0], LANES), F32)
        update([(jnp.concatenate([kn_ref[...], pad], 0), jnp.concatenate([vn_ref[...], pad], 0),
                 bias_ref[nc * npg])])
        o = acc_sc[...] * _as_column(1.0 / l_sc[...])
        half = o.shape[0] // 2
        o = o[:half] - lam_ref[0, 0] * o[half:]
        o_ref[...] = o * lax.rsqrt(jnp.mean(o * o, -1, keepdims=True) + LN_EPS) * subln_ref[...] * post


def _diff_sample_attn(q2d, k2d, v2d, cache_k, cache_v, slot, page_table, bias_d, lam, lam_init, subln, db, t):
    n_phys, page = cache_k.shape[1], cache_k.shape[2]
    n_pages = page_table.shape[1]
    past = n_pages * page
    npg = min(PAGES_PER_STEP, n_pages)
    assert n_pages % npg == 0
    prow = page * DIFF_KV_HEADS
    ck = cache_k.reshape(cache_k.shape[0], n_phys, prow, LANES)
    cv = cache_v.reshape(cache_v.shape[0], n_phys, prow, LANES)
    cols = 2 * DIFF_HEADS * t
    assert cols == LANES
    npos = (n_pages + 1) * page
    f = _toeplitz(jnp.concatenate([_neg((DIFF_HEADS, page - 1)), bias_d[:, :past + t]], 1), t, npos)
    f = f.reshape(DIFF_HEADS, t, n_pages + 1, page).transpose(2, 3, 0, 1)
    own = (jnp.arange(DIFF_KV_HEADS)[:, None] == (jnp.arange(DIFF_HEADS) // (DIFF_HEADS // DIFF_KV_HEADS))[None, :])
    table = jnp.where(own[None, None, :, None, :, None], f[:, :, None, None, :, :] * LOG2E, NEG_INF)
    table = jnp.broadcast_to(table, (n_pages + 1, page, DIFF_KV_HEADS, 2, DIFF_HEADS, t))
    table = table.reshape(n_pages + 1, prow, cols)

    def page_spec(i):
        return pl.BlockSpec((None, None, prow, LANES), lambda b, c, pt: (slot, pt[b, c * npg + i], 0, 0))

    grid_spec = pltpu.PrefetchScalarGridSpec(
        num_scalar_prefetch=1,
        grid=(db, n_pages // npg),
        in_specs=[pl.BlockSpec(memory_space=pltpu.SMEM),
                  pl.BlockSpec((None, t, DIFF_HEADS * LANES), lambda b, c, pt: (b, 0, 0)),
                  pl.BlockSpec((None, t * DIFF_KV_HEADS, LANES), lambda b, c, pt: (b, 0, 0)),
                  pl.BlockSpec((None, t * DIFF_KV_HEADS, LANES), lambda b, c, pt: (b, 0, 0)),
                  pl.BlockSpec(table.shape, lambda b, c, pt: (0, 0, 0)),
                  pl.BlockSpec((1, LANES), lambda b, c, pt: (0, 0))]
        + [page_spec(i) for i in range(npg)] + [page_spec(i) for i in range(npg)],
        out_specs=pl.BlockSpec((None, cols // 2, LANES), lambda b, c, pt: (b, 0, 0)),
        scratch_shapes=[pltpu.VMEM((cols, LANES), BF16),
                        pltpu.VMEM((1, cols), F32), pltpu.VMEM((1, cols), F32), pltpu.VMEM((cols, LANES), F32)])
    o = pl.pallas_call(
        functools.partial(_diff_sample_kernel, npg=npg, qmul=DIFF_DH ** -0.5 * LOG2E, post=1.0 - lam_init),
        grid_spec=grid_spec,
        out_shape=jax.ShapeDtypeStruct((db, cols // 2, LANES), F32),
        compiler_params=_params("arbitrary", "arbitrary"),
        name="diff_sample_attn",
    )(page_table, lam.reshape(1, 1), q2d.reshape(db, t, -1), k2d.reshape(db, t * DIFF_KV_HEADS, LANES),
      v2d.reshape(db, t * DIFF_KV_HEADS, LANES), table, subln.reshape(1, LANES),
      *([ck] * npg), *([cv] * npg))
    return o.reshape(db, DIFF_HEADS, t, LANES).transpose(0, 2, 1, 3).reshape(db * t, DIFF_HEADS * LANES)


def _dil_prompt_kernel(q_ref, kp_ref, kc_ref, vp_ref, vc_ref, bias_ref, o_ref, lse_ref, *, scale):
    qi = pl.program_id(2)
    tq = q_ref.shape[0]
    q = q_ref[...]
    k = jnp.concatenate([kp_ref[...], kc_ref[...]], 0).astype(BF16)
    v = jnp.concatenate([vp_ref[...], vc_ref[...]], 0).astype(BF16)
    col = _lane((tq, 2 * tq), 1)
    prev_ok = (col >= tq) | (qi > 0)
    first = _lane((tq, LANES), 1) < DIL_DH
    for p in range(DIL_HEADS // 2):
        sl = slice(p * LANES, (p + 1) * LANES)
        qp, kp, vp = q[:, sl], k[:, sl], v[:, sl]
        outs, lses = [], []
        for a in range(2):
            qa = jnp.where(first, qp, jnp.zeros_like(qp)) if a == 0 else jnp.where(first, jnp.zeros_like(qp), qp)
            s = _dot_nt(qa, kp) * scale + bias_ref[2 * p + a]
            s = jnp.where(prev_ok, s, NEG_INF)
            m = jnp.max(s, -1, keepdims=True)
            e = jnp.exp(s - m)
            l = jnp.sum(e, -1, keepdims=True)
            outs.append(_dot(e.astype(BF16), vp) / l)
            lses.append(m + jnp.log(l))
        o_ref[:, sl] = jnp.where(first, outs[0], outs[1])
        lse_ref[:, sl] = jnp.where(first, jnp.broadcast_to(lses[0], (tq, LANES)),
                                   jnp.broadcast_to(lses[1], (tq, LANES)))


def _dil_prompt_group(q2d, kv2d, bias_d, window, dil, b, s):
    ln = s // dil
    band = window // dil
    tq = min(DIL_TILE, ln)
    assert band <= tq or ln == tq
    nb = ln // tq
    hw = DIL_HEADS * DIL_DH
    wvec = jnp.concatenate([_neg((DIL_HEADS, tq - 1)), bias_d[:, :band * dil + 1:dil],
                            _neg((DIL_HEADS, 2 * tq - band - 1))], 1)
    table = _toeplitz(wvec, tq, 2 * tq)
    qv = q2d.reshape(b, ln, dil * hw)
    kvv = kv2d.reshape(b, ln, dil * 2 * hw)

    def kv_spec(prev, val):
        return pl.BlockSpec((None, tq, hw),
                            lambda bb, c, qi: (bb, jnp.maximum(qi - 1, 0) if prev else qi, 2 * c + val))

    o, lse = pl.pallas_call(
        functools.partial(_dil_prompt_kernel, scale=DIL_DH ** -0.5),
        grid=(b, dil, nb),
        in_specs=[pl.BlockSpec((None, tq, hw), lambda bb, c, qi: (bb, qi, c)),
                  kv_spec(True, 0), kv_spec(False, 0), kv_spec(True, 1), kv_spec(False, 1),
                  pl.BlockSpec(table.shape, lambda bb, c, qi: (0, 0, 0))],
        out_specs=[pl.BlockSpec((None, tq, hw), lambda bb, c, qi: (bb, qi, c))] * 2,
        out_shape=[jax.ShapeDtypeStruct((b, ln, dil * hw), F32)] * 2,
        compiler_params=_params("arbitrary", "arbitrary", "arbitrary"),
        name="dil_prompt_attn",
    )(qv, kvv, kvv, kvv, kvv, table)
    return o.reshape(b * s, hw), lse.reshape(b * s, hw)


def _dil_sample_kernel(q0, q1, q2, kv0, kv1, kv2, b0, b1, b2, o_ref, *, scale):
    t = q0.shape[0]
    hw = DIL_HEADS * DIL_DH
    rows = DIL_HEADS * t
    diag = (_lane((rows, hw), 1) // DIL_DH) == (_lane((rows, hw), 0) // t)
    outs, lses = [], []
    for q_ref, kv_ref, b_ref in ((q0, kv0, b0), (q1, kv1, b1), (q2, kv2, b2)):
        q = q_ref[...]
        qbd = jnp.where(diag, jnp.concatenate([q] * DIL_HEADS, 0), 0.0).astype(BF16)
        s = _dot_nt(qbd, kv_ref[:, :hw]) * scale + b_ref[...]
        m = jnp.max(s, -1, keepdims=True)
        e = jnp.exp(s - m)
        l = jnp.sum(e, -1, keepdims=True)
        outs.append(_dot(e.astype(BF16), kv_ref[:, hw:]) / l)
        lses.append(m + jnp.log(l))
    mx = jnp.maximum(jnp.maximum(lses[0], lses[1]), lses[2])
    es = [jnp.exp(x - mx) for x in lses]
    o = (es[0] * outs[0] + es[1] * outs[1] + es[2] * outs[2]) / (es[0] + es[1] + es[2])
    o = jnp.where(diag, o, 0.0)
    o_ref[...] = jnp.sum(o.reshape(DIL_HEADS, t, hw), 0)


def _dil_sample_attn(qs, kv_news, bufs, bias_d, past, db, t):
    hw = DIL_HEADS * DIL_DH
    rows = DIL_HEADS * t
    kvs, tables = [], []
    for (window, dil), kvn, buf in zip(DIL_PATTERNS, kv_news, bufs):
        wb = buf.shape[1]
        pad = -(wb + t) % LANES
        kvs.append(jnp.concatenate([buf.reshape(db, wb, 2 * hw).astype(BF16),
                                    kvn.reshape(db, t, 2 * hw).astype(BF16),
                                    jnp.zeros((db, pad, 2 * hw), BF16)], 1))
        wp = wb + t + pad
        f = _toeplitz(jnp.concatenate([_neg((DIL_HEADS, wp - 1 - wb)), bias_d[:, :wb + t]], 1), t, wp)
        delta = wb + jnp.arange(t)[:, None] - jnp.arange(wp)[None, :]
        ok = (delta >= 0) & (delta % dil == 0) & (delta <= window) & (past - delta + jnp.arange(t)[:, None] >= 0)
        tables.append(jnp.where(ok[None], f, NEG_INF).reshape(rows, wp))
    o = pl.pallas_call(
        functools.partial(_dil_sample_kernel, scale=DIL_DH ** -0.5),
        grid=(db,),
        in_specs=[pl.BlockSpec((None, t, hw), lambda b: (b, 0, 0))] * 3
        + [pl.BlockSpec((None,) + kv.shape[1:], lambda b: (b, 0, 0)) for kv in kvs]
        + [_resident(tb.shape) for tb in tables],
        out_specs=pl.BlockSpec((None, t, hw), lambda b: (b, 0, 0)),
        out_shape=jax.ShapeDtypeStruct((db, t, hw), F32),
        compiler_params=_params("arbitrary"),
        name="dil_sample_attn",
    )(*[q.reshape(db, t, hw) for q in qs], *kvs, *tables)
    return o.reshape(db * t, hw)


def _mla_proj_kernel(x_ref, sh_ref, sc_ref, cq_ref, sq_ref, ck_ref, sk_ref, win_ref, qn_ref, kvn_ref,
                     wqa_ref, wqb_ref, *rest, with_kv, qmul):
    if with_kv:
        wk_ref, wv_ref, ckv_ref, kr_ref, q_ref, k_ref, v_ref = rest
    else:
        ckv_ref, kr_ref, q_ref = rest
    tb, ts, d = x_ref.shape
    rows = tb * ts
    h = x_ref[...] * (1.0 + sc_ref[...]) + sh_ref[...]
    h = h.reshape(rows, d).astype(BF16)
    proj = _dot(h, win_ref[...])
    r0, r1 = MLA_Q_RANK, MLA_Q_RANK + MLA_KV_RANK

    def rms(z, g):
        return z * lax.rsqrt(jnp.mean(z * z, -1, keepdims=True) + LN_EPS) * g

    cq = rms(proj[:, :r0], qn_ref[...]).astype(BF16)
    ckv = rms(proj[:, r0:r1], kvn_ref[...])
    kra = proj[:, r1:r1 + LANES].reshape(tb, ts, LANES)
    krb = proj[:, r1 + LANES:r1 + 2 * LANES].reshape(tb, ts, LANES)
    kr = (kra * ck_ref[...] + krb * sk_ref[...]).reshape(rows, LANES)
    ckv_ref[...] = ckv
    kr_ref[...] = kr[:, :MLA_ROPE]
    nq = q_ref.shape[1]
    cqt = jnp.concatenate([cq_ref[...]] * MLA_HEADS, -1)
    sqt = jnp.concatenate([sq_ref[...]] * MLA_HEADS, -1)
    qa = _dot(cq, wqa_ref[...]).reshape(tb, ts, nq)
    qb = _dot(cq, wqb_ref[...]).reshape(tb, ts, nq)
    q_ref[...] = ((qa * cqt + qb * sqt) * qmul).reshape(rows, nq).astype(q_ref.dtype)
    if with_kv:
        kcat = jnp.concatenate([ckv.astype(BF16), kr.astype(BF16)], 1)
        k_ref[...] = _dot(kcat, wk_ref[...]).astype(k_ref.dtype)
        v_ref[...] = _dot_nt(wv_ref[...], ckv.astype(BF16)).astype(v_ref.dtype)


def _mla_weights(w_in, w_uq, w_uk, w_uv):
    d = w_in.shape[0]
    r0, r1 = MLA_Q_RANK, MLA_Q_RANK + MLA_KV_RANK
    half = MLA_ROPE // 2
    zpad = jnp.zeros((d, LANES - MLA_ROPE), F32)
    kr_w = w_in[:, r1:]
    kr_sw = jnp.concatenate([kr_w[:, half:], kr_w[:, :half]], 1)
    win = jnp.concatenate([w_in[:, :r1], kr_w, zpad, kr_sw, zpad], 1).astype(BF16)
    hd = MLA_NOPE + MLA_ROPE
    wq = w_uq.reshape(MLA_Q_RANK, MLA_HEADS, hd)
    z32 = jnp.zeros((MLA_Q_RANK, MLA_HEADS, LANES - hd), F32)
    z64 = jnp.zeros((MLA_Q_RANK, MLA_HEADS, MLA_NOPE), F32)
    wqa = jnp.concatenate([wq, z32], -1).reshape(MLA_Q_RANK, MLA_HEADS * LANES).astype(BF16)
    wqb = jnp.concatenate([z64, wq[..., MLA_NOPE + half:], wq[..., MLA_NOPE:MLA_NOPE + half], z32], -1)
    wqb = wqb.reshape(MLA_Q_RANK, MLA_HEADS * LANES).astype(BF16)
    top = jnp.concatenate([w_uk, jnp.zeros((MLA_KV_RANK, MLA_HEADS, LANES - MLA_NOPE), F32)], -1)
    eye = jnp.eye(MLA_ROPE, dtype=F32)[:, None, :]
    mid = jnp.concatenate([jnp.zeros((MLA_ROPE, MLA_HEADS, MLA_NOPE), F32),
                           jnp.broadcast_to(eye, (MLA_ROPE, MLA_HEADS, MLA_ROPE)),
                           jnp.zeros((MLA_ROPE, MLA_HEADS, LANES - hd), F32)], -1)
    bot = jnp.zeros((LANES - MLA_ROPE, MLA_HEADS, LANES), F32)
    wk = jnp.concatenate([top, mid, bot], 0).reshape(MLA_KV_RANK + LANES, MLA_HEADS * LANES).astype(BF16)
    wv = w_uv.reshape(MLA_KV_RANK, MLA_HEADS * MLA_V).astype(BF16)
    a_top = jnp.concatenate([w_uk.transpose(1, 2, 0),
                             jnp.zeros((MLA_HEADS, MLA_NOPE, LANES), F32)], -1)
    a_mid = jnp.concatenate([jnp.zeros((MLA_ROPE, MLA_KV_RANK), F32), jnp.eye(MLA_ROPE, dtype=F32),
                             jnp.zeros((MLA_ROPE, LANES - MLA_ROPE), F32)], -1)
    a_mid = jnp.broadcast_to(a_mid[None], (MLA_HEADS, MLA_ROPE, MLA_KV_RANK + LANES))
    a_bot = jnp.zeros((MLA_HEADS, LANES - hd, MLA_KV_RANK + LANES), F32)
    wabs = jnp.concatenate([a_top, a_mid, a_bot], 1).reshape(MLA_HEADS * LANES, MLA_KV_RANK + LANES).astype(BF16)
    return win, wqa, wqb, wk, wv, wabs


def _mla_proj(x, shift, scale, pos, win, qn, kvn, wqa, wqb, wk, wv, with_kv, qmul):
    b, s, d = x.shape
    tb, ts = _row_blocks(b, s)
    rows = tb * ts
    cq, sq, ck, sk = _rope_tables(pos)
    nq = MLA_HEADS * LANES
    in_specs = [_x_spec(tb, ts, s, d), _mod_spec(tb, ts, s, d), _mod_spec(tb, ts, s, d),
                _pos_spec(ts, s, LANES), _pos_spec(ts, s, LANES), _pos_spec(ts, s, LANES), _pos_spec(ts, s, LANES),
                _resident(win.shape), _resident((1, MLA_Q_RANK)), _resident((1, MLA_KV_RANK)),
                _resident(wqa.shape), _resident(wqb.shape)]
    args = [x, shift, scale, cq, sq, ck, sk, win, qn.reshape(1, -1), kvn.reshape(1, -1), wqa, wqb]
    out_specs = [_rows_spec(rows, MLA_KV_RANK), _rows_spec(rows, MLA_ROPE), _rows_spec(rows, nq)]
    out_shape = [jax.ShapeDtypeStruct((b * s, MLA_KV_RANK), F32), jax.ShapeDtypeStruct((b * s, MLA_ROPE), F32),
                 jax.ShapeDtypeStruct((b * s, nq), BF16 if with_kv else F32)]
    if with_kv:
        assert tb == 1
        nst = s // ts
        nv = MLA_HEADS * MLA_V
        wv_t = wv.T
        in_specs += [_resident(wk.shape), _resident(wv_t.shape)]
        args += [wk, wv_t]
        out_specs += [_rows_spec(rows, nq), pl.BlockSpec((None, nv, ts), lambda i: (i // nst, 0, i % nst))]
        out_shape += [jax.ShapeDtypeStruct((b * s, nq), BF16), jax.ShapeDtypeStruct((b, nv, s), BF16)]
    return pl.pallas_call(
        functools.partial(_mla_proj_kernel, with_kv=with_kv, qmul=qmul),
        grid=(b * s // rows,),
        in_specs=in_specs, out_specs=out_specs, out_shape=out_shape,
        compiler_params=_params("arbitrary"),
        name="mla_proj",
    )(*args)


def _mla_prompt_kernel(qi_ref, ki_ref, q_ref, k_ref, vt_ref, o_ref, m_sc, l_sc, acc_sc):
    pair = pl.program_id(2)
    qi, ki = qi_ref[pair], ki_ref[pair]
    t = q_ref.shape[0]
    n_heads = q_ref.shape[1] // LANES

    @pl.when(ki == 0)
    def _():
        _init_stats(m_sc, l_sc, acc_sc)

    def update(diagonal):
        scores = [_dot_nt(k_ref[:, a * LANES:(a + 1) * LANES], q_ref[:, a * LANES:(a + 1) * LANES])
                  for a in range(n_heads)]
        for a in range(n_heads):
            s = scores[a]
            if diagonal:
                s = jnp.where(_lane((t, t), 0) <= _lane((t, t), 1), s, NEG_INF)
            _softmax_step([s], [vt_ref[(a // 2) * LANES:(a // 2 + 1) * LANES, :]], m_sc, l_sc, acc_sc, a)

    pl.when(ki < qi)(functools.partial(update, False))
    pl.when(ki == qi)(functools.partial(update, True))

    @pl.when(ki == qi)
    def _():
        first = _lane((LANES, t), 0) < MLA_V
        for p in range(n_heads // 2):
            o = jnp.where(first, acc_sc[2 * p] * (1.0 / l_sc[2 * p]), acc_sc[2 * p + 1] * (1.0 / l_sc[2 * p + 1]))
            o_ref[:, p * LANES:(p + 1) * LANES] = o.T.astype(o_ref.dtype)


def _mla_prompt_attn(q2d, k2d, v_t, b, s):
    t = min(MLA_TILE, s)
    nt = s // t
    hs = MLA_HEADS_PER_STEP
    qis, kis = _causal_pairs(nt)
    grid_spec = pltpu.PrefetchScalarGridSpec(
        num_scalar_prefetch=2,
        grid=(b, MLA_HEADS // hs, qis.shape[0]),
        in_specs=[pl.BlockSpec((None, t, hs * LANES), lambda bb, h, p, qa, ka: (bb, qa[p], h)),
                  pl.BlockSpec((None, t, hs * LANES), lambda bb, h, p, qa, ka: (bb, ka[p], h)),
                  pl.BlockSpec((None, hs * MLA_V, t), lambda bb, h, p, qa, ka: (bb, h, ka[p]))],
        out_specs=pl.BlockSpec((None, t, hs * MLA_V), lambda bb, h, p, qa, ka: (bb, qa[p], h)),
        scratch_shapes=[pltpu.VMEM((hs, 1, t), F32), pltpu.VMEM((hs, 1, t), F32), pltpu.VMEM((hs, LANES, t), F32)])
    return pl.pallas_call(
        _mla_prompt_kernel,
        grid_spec=grid_spec,
        out_shape=jax.ShapeDtypeStruct((b, s, MLA_HEADS * MLA_V), BF16),
        compiler_params=_params("arbitrary", "arbitrary", "arbitrary"),
        name="mla_prompt_attn",
    )(qis, kis, q2d.reshape(b, s, -1), k2d.reshape(b, s, -1), v_t).reshape(b * s, -1)


def _mla_sample_kernel(pt_ref, q_ref, cn_ref, rn_ref, wabs_ref, wv_ref, *rest, npg, qmul):
    c_refs, r_refs = rest[:npg], rest[npg:2 * npg]
    o_ref, qabs, rpad, m_sc, l_sc, acc_sc = rest[2 * npg:]
    c = pl.program_id(1)
    nc = pl.num_programs(1)
    t = q_ref.shape[0]
    cols = MLA_HEADS * t
    page = c_refs[0].shape[0]

    @pl.when(c == 0)
    def _():
        q = q_ref[...]
        qs = jnp.concatenate([q[:, h * LANES:(h + 1) * LANES] for h in range(MLA_HEADS)], 0)
        rh = _lane((cols, LANES), 0) // t
        qbd = jnp.concatenate([jnp.where(rh == h, qs, 0.0) for h in range(MLA_HEADS)], 1).astype(BF16)
        qabs[...] = (_dot(qbd, wabs_ref[...]) * qmul).astype(BF16)
        rpad[...] = jnp.zeros(rpad.shape, F32)
        _init_stats(m_sc, l_sc, acc_sc)

    def update(lats, valid):
        scores, latbs = [], []
        for lat, j in lats:
            latb = lat.astype(BF16)
            s = _dot_nt(jnp.concatenate([latb, rpad[j].astype(BF16)], 1), qabs[...])
            scores.append(s if valid is None else jnp.where(valid, s, NEG_INF))
            latbs.append(latb)
        _decode_update(scores, latbs, m_sc, l_sc, acc_sc)

    for i in range(npg):
        rpad[i, :, :MLA_ROPE] = r_refs[i][...]
    update([(c_refs[i][...], i) for i in range(npg)], None)

    @pl.when(c == nc - 1)
    def _():
        rpad[0] = jnp.zeros(rpad.shape[1:], F32)
        rpad[0, :t, :MLA_ROPE] = rn_ref[...]
        lat = jnp.concatenate([cn_ref[...], jnp.zeros((page - t, MLA_KV_RANK), F32)], 0)
        update([(lat, 0)], _lane((page, cols), 0) <= (_lane((page, cols), 1) % t))
        o_lat = (acc_sc[...] * _as_column(1.0 / l_sc[...])).astype(BF16)
        of = _dot(o_lat, wv_ref[...])
        n = of.shape[1]
        own = (_lane((cols, n), 1) // MLA_V) == (_lane((cols, n), 0) // t)
        o_ref[...] = jnp.sum(jnp.where(own, of, 0.0).reshape(MLA_HEADS, t, n), 0)


def _mla_sample_attn(q2d, ckv_new, kr_new, cache_ckv, cache_krope, slot, page_table, wabs, wv, db, t):
    page = cache_ckv.shape[2]
    n_pages = page_table.shape[1]
    npg = min(PAGES_PER_STEP, n_pages)
    assert n_pages % npg == 0
    cols = MLA_HEADS * t
    nv = MLA_HEADS * MLA_V

    def page_spec(w, i):
        return pl.BlockSpec((None, None, page, w), lambda b, c, pt: (slot, pt[b, c * npg + i], 0, 0))

    grid_spec = pltpu.PrefetchScalarGridSpec(
        num_scalar_prefetch=1,
        grid=(db, n_pages // npg),
        in_specs=[pl.BlockSpec((None, t, MLA_HEADS * LANES), lambda b, c, pt: (b, 0, 0)),
                  pl.BlockSpec((None, t, MLA_KV_RANK), lambda b, c, pt: (b, 0, 0)),
                  pl.BlockSpec((None, t, MLA_ROPE), lambda b, c, pt: (b, 0, 0)),
                  pl.BlockSpec(wabs.shape, lambda b, c, pt: (0, 0)),
                  pl.BlockSpec(wv.shape, lambda b, c, pt: (0, 0))]
        + [page_spec(MLA_KV_RANK, i) for i in range(npg)] + [page_spec(MLA_ROPE, i) for i in range(npg)],
        out_specs=pl.BlockSpec((None, t, nv), lambda b, c, pt: (b, 0, 0)),
        scratch_shapes=[pltpu.VMEM((cols, MLA_KV_RANK + LANES), BF16), pltpu.VMEM((npg, page, LANES), F32),
                        pltpu.VMEM((1, cols), F32), pltpu.VMEM((1, cols), F32),
                        pltpu.VMEM((cols, MLA_KV_RANK), F32)])
    o = pl.pallas_call(
        functools.partial(_mla_sample_kernel, npg=npg, qmul=(MLA_NOPE + MLA_ROPE) ** -0.5 * LOG2E),
        grid_spec=grid_spec,
        out_shape=jax.ShapeDtypeStruct((db, t, nv), F32),
        compiler_params=_params("arbitrary", "arbitrary"),
        name="mla_sample_attn",
    )(page_table, q2d.reshape(db, t, -1), ckv_new.reshape(db, t, -1), kr_new.reshape(db, t, -1), wabs, wv,
      *([cache_ckv] * npg), *([cache_krope] * npg))
    return o.reshape(db * t, nv)


def _router_kernel(x_ref, sh_ref, sc_ref, whi_ref, wlo_ref, b_ref, h_ref, lg_ref):
    tb, ts, d = x_ref.shape
    h = (x_ref[...] * (1.0 + sc_ref[...]) + sh_ref[...]).reshape(tb * ts, d)
    hi = h.astype(BF16)
    lo = (h - hi.astype(F32)).astype(BF16)
    h_ref[...] = hi
    lg = _dot(hi, whi_ref[...]) + _dot(hi, wlo_ref[...]) + _dot(lo, whi_ref[...]) + b_ref[...]
    lane = _lane(lg.shape, 1)
    far = jnp.int32(LANES)

    def top(vals):
        v = jnp.max(vals, -1, keepdims=True)
        return v, jnp.min(jnp.where(vals == v, lane, far), -1, keepdims=True)

    is_group = lane < MOE_GROUPS
    gmax, gsel = top(jnp.where(is_group, lg, NEG_INF))
    gprob = 1.0 / jnp.sum(jnp.where(is_group, jnp.exp(lg - gmax), 0.0), -1, keepdims=True)
    lo_lane = MOE_GROUPS + MOE_EXPERTS_PER_GROUP * gsel
    el = jnp.where((lane >= lo_lane) & (lane < lo_lane + MOE_EXPERTS_PER_GROUP), lg, NEG_INF)
    v1, i1 = top(el)
    v2, i2 = top(jnp.where(lane == i1, NEG_INF, el))
    ex = jnp.exp(v2 - v1)
    g1 = gprob / (1.0 + ex)
    g2 = g1 * ex
    e1 = (i1 - MOE_GROUPS).astype(F32)
    e2 = (i2 - MOE_GROUPS).astype(F32)
    lg_ref[...] = jnp.where(lane == 0, e1, jnp.where(lane == 1, e2, jnp.where(lane == 2, g1, jnp.where(lane == 3, g2, 0.0))))


def _router(x, shift, scale, whi, wlo, bias):
    b, s, d = x.shape
    tb, ts = _row_blocks(b, s)
    rows = tb * ts
    return pl.pallas_call(
        _router_kernel,
        grid=(b * s // rows,),
        in_specs=[_x_spec(tb, ts, s, d), _mod_spec(tb, ts, s, d), _mod_spec(tb, ts, s, d),
                  _resident(whi.shape), _resident(wlo.shape), _resident(bias.shape)],
        out_specs=[_rows_spec(rows, d), _rows_spec(rows, LANES)],
        out_shape=[jax.ShapeDtypeStruct((b * s, d), BF16), jax.ShapeDtypeStruct((b * s, LANES), F32)],
        compiler_params=_params("arbitrary"),
        name="router",
    )(x, shift, scale, whi, wlo, bias)


def _expert_kernel(blk_ref, exp_ref, first_ref, fresh_ref, lo_ref, hi_ref, n_ref, x_ref, w1_ref, w3_ref, w2_ref,
                   y_ref, w1_sc, w3_sc, w2_sc):
    w = pl.program_id(0)

    @pl.when(w < n_ref[0])
    def _():
        @pl.when(fresh_ref[w] == 1)
        def _():
            w1_sc[...] = w1_ref[...].astype(BF16)
            w3_sc[...] = w3_ref[...].astype(BF16)
            w2_sc[...] = w2_ref[...].astype(BF16)

        x = x_ref[...]
        a = _dot(x, w1_sc[...])
        u = (a * jax.nn.sigmoid(a)) * _dot(x, w3_sc[...])
        y = _dot(u.astype(BF16), w2_sc[...]).astype(y_ref.dtype)
        row = blk_ref[w] * x.shape[0] + _lane((x.shape[0], 1), 0)
        mine = (row >= lo_ref[w]) & (row < hi_ref[w])

        @pl.when(first_ref[w] == 1)
        def _():
            y_ref[...] = jnp.where(mine, y, jnp.zeros_like(y))

        @pl.when(first_ref[w] == 0)
        def _():
            y_ref[...] = jnp.where(mine, y, y_ref[...])


def _experts(xs, items, w1, w3, w2, layer):
    a, d = xs.shape
    ff = w1.shape[3]
    n_items = items[0].shape[0]
    grid_spec = pltpu.PrefetchScalarGridSpec(
        num_scalar_prefetch=7,
        grid=(n_items,),
        in_specs=[pl.BlockSpec((MOE_ROWS, d), lambda w, blk, ex, *_: (blk[w], 0)),
                  pl.BlockSpec((None, None, d, ff), lambda w, blk, ex, *_: (layer, ex[w], 0, 0)),
                  pl.BlockSpec((None, None, d, ff), lambda w, blk, ex, *_: (layer, ex[w], 0, 0)),
                  pl.BlockSpec((None, None, ff, d), lambda w, blk, ex, *_: (layer, ex[w], 0, 0))],
        out_specs=pl.BlockSpec((MOE_ROWS, d), lambda w, blk, ex, *_: (blk[w], 0)),
        scratch_shapes=[pltpu.VMEM((d, ff), BF16), pltpu.VMEM((d, ff), BF16), pltpu.VMEM((ff, d), BF16)])
    return pl.pallas_call(
        _expert_kernel,
        grid_spec=grid_spec,
        out_shape=jax.ShapeDtypeStruct((a, d), BF16),
        compiler_params=_params("arbitrary"),
        name="experts",
    )(*items, xs, w1, w3, w2)


def _combine_ln_kernel(y0_ref, y1_ref, r_ref, x_ref, gate_ref, g_ref, b_ref, o_ref, *, alpha):
    tb, ts, d = x_ref.shape
    r = r_ref[...]
    y = y0_ref[...].astype(F32) * r[:, 2:3] + y1_ref[...].astype(F32) * r[:, 3:4]
    z = alpha * x_ref[...] + gate_ref[...] * y.reshape(tb, ts, d)
    o_ref[...] = _layernorm(z, g_ref[...], b_ref[...])


def _combine_ln(y0, y1, route, row_off, x, gate, g, b, alpha):
    bb, s, d = x.shape
    tb, ts = _row_blocks(bb, s)
    rows = tb * ts
    assert row_off % rows == 0
    off = row_off // rows
    return pl.pallas_call(
        functools.partial(_combine_ln_kernel, alpha=alpha),
        grid=(bb * s // rows,),
        in_specs=[_rows_spec(rows, d, off), _rows_spec(rows, d, off), _rows_spec(rows, LANES, off),
                  _x_spec(tb, ts, s, d), _mod_spec(tb, ts, s, d), _resident((1, 1, d)), _resident((1, 1, d))],
        out_specs=_x_spec(tb, ts, s, d),
        out_shape=jax.ShapeDtypeStruct(x.shape, F32),
        compiler_params=_params("arbitrary"),
        name="combine_ln",
    )(y0, y1, route, x, gate, g.reshape(1, 1, d), b.reshape(1, 1, d))


def _dispatch(eid):
    n = eid.shape[0]
    a = n * MOE_TOPK
    n_blocks = -(-a // MOE_ROWS)
    flat_e = eid.reshape(a)
    order = jnp.argsort(flat_e).astype(jnp.int32)
    inv = jnp.argsort(order).astype(jnp.int32).reshape(n, MOE_TOPK)
    counts = jnp.sum((flat_e[:, None] == jnp.arange(MOE_EXPERTS)[None, :]).astype(jnp.int32), 0)
    end = jnp.cumsum(counts)
    start = end - counts
    b0 = start // MOE_ROWS
    nb = jnp.where(counts > 0, (end - 1) // MOE_ROWS - b0 + 1, 0)
    cum = jnp.cumsum(nb)
    n_items = n_blocks + MOE_EXPERTS - 1
    w = jnp.arange(n_items)
    ex = jnp.minimum(jnp.sum((cum[None, :] <= w[:, None]).astype(jnp.int32), 1), MOE_EXPERTS - 1)
    live = w < cum[-1]
    blk = jnp.where(live, b0[ex] + w - (cum - nb)[ex], n_blocks - 1)
    ex = jnp.where(live, ex, ex[jnp.maximum(cum[-1] - 1, 0)])
    prev_blk = jnp.concatenate([jnp.full((1,), -1, blk.dtype), blk[:-1]])
    prev_ex = jnp.concatenate([jnp.full((1,), -1, ex.dtype), ex[:-1]])
    items = [blk, ex, blk != prev_blk, ex != prev_ex, start[ex], end[ex], cum[-1:]]
    return order, inv, [x.astype(jnp.int32) for x in items]


def _moe_layer(yp, ys, mod_p, mod_s, g, b, wg, bg, we, be, w1, w3, w2, layer, alpha):
    d = yp.shape[-1]
    npad = LANES - MOE_GROUPS - MOE_EXPERTS
    wr = jnp.concatenate([wg, we, jnp.zeros((d, npad), F32)], 1)
    whi = wr.astype(BF16)
    wlo = (wr - whi.astype(F32)).astype(BF16)
    br = jnp.concatenate([bg, be, jnp.zeros((npad,), F32)]).reshape(1, LANES)
    hp, rp = _router(yp, mod_p[3], mod_p[4], whi, wlo, br)
    hs, rs = _router(ys, mod_s[3], mod_s[4], whi, wlo, br)
    h = jnp.concatenate([hp, hs], 0)
    route = jnp.concatenate([rp, rs], 0)
    order, inv, items = _dispatch(route[:, :MOE_TOPK].astype(jnp.int32))
    yb = _experts(h[order // MOE_TOPK], items, w1, w3, w2, layer)
    y0, y1 = yb[inv[:, 0]], yb[inv[:, 1]]
    yp = _combine_ln(y0, y1, route, 0, yp, mod_p[5], g, b, alpha)
    ys = _combine_ln(y0, y1, route, hp.shape[0], ys, mod_s[5], g, b, alpha)
    return yp, ys


def kernel(x_prompt, x_sample, cache_diff_k, cache_diff_v, cache_dil_kv_g0, cache_dil_kv_g1, cache_dil_kv_g2, cache_mla_ckv, cache_mla_krope, page_table, c_prompt, c_sample, rel_bias, ada_w, ada_b, ln_g, ln_b, diff_w_in, diff_lam, diff_subln, diff_w_out, dil_w_in, dil_w_out, mla_w_in, mla_q_norm, mla_kv_norm, mla_w_uq, mla_w_uk, mla_w_uv, mla_w_out, moe_wg, moe_bg, moe_we, moe_be, moe_w1, moe_w3, moe_w2):
    depth = ada_w.shape[0]
    b, s, d = x_prompt.shape
    db, t, _ = x_sample.shape
    past = page_table.shape[1] * cache_diff_k.shape[2]
    alpha = (2 * depth) ** 0.25
    dil_cache = (cache_dil_kv_g0, cache_dil_kv_g1, cache_dil_kv_g2)
    bias_d = _bias_by_dist(rel_bias, max(s, past + t))

    mods = _ada(jnp.concatenate([c_prompt, c_sample], 0), ada_w, ada_b)
    yp, ys = x_prompt, x_sample
    dk_p, dv_p, dk_s, dv_s = [], [], [], []
    dl_p = [[] for _ in DIL_PATTERNS]
    dl_s = [[] for _ in DIL_PATTERNS]
    ck_p, kr_p, ck_s, kr_s = [], [], [], []
    for i in range(depth):
        kind, j = i % N_MIXERS, i // N_MIXERS
        mod_p = [mods[i, :b, None, m * d:(m + 1) * d] for m in range(6)]
        mod_s = [mods[i, b:, None, m * d:(m + 1) * d] for m in range(6)]
        if kind == 0:
            nq = DIFF_HEADS * 2 * DIFF_DH
            nk = DIFF_KV_HEADS * 2 * DIFF_DH
            qkv = ((0, nq, 1.0), (nq, nq + nk, 1.0), (nq + nk, nq + 2 * nk, 1.0))
            splits_p = ((0, nq, DIFF_DH ** -0.5 * LOG2E),) + qkv[1:] + ((nq, nq + nk, 1.0),)
            w_in = diff_w_in[j].astype(BF16)
            lam_init = 0.8 - 0.6 * math.exp(-0.3 * i)
            lp = diff_lam[j].astype(F32)
            lam = jnp.exp(jnp.sum(lp[0] * lp[1])) - jnp.exp(jnp.sum(lp[2] * lp[3])) + lam_init
            qp, kp, vp, kbp, vtp = _proj(yp, mod_p[0], mod_p[1], w_in, splits_p, (BF16, F32, F32, BF16),
                                         wt=w_in[:, nq + nk:].T)
            qs, ks, vs = _proj(ys, mod_s[0], mod_s[1], w_in, qkv, (F32, F32, F32))
            o_p = _diff_prompt_attn(qp, kbp, vtp, bias_d, lam, lam_init, diff_subln[j], b, s)
            o_s = _diff_sample_attn(qs, ks, vs, cache_diff_k, cache_diff_v, j, page_table, bias_d, lam, lam_init,
                                    diff_subln[j], db, t)
            dk_p.append(kp.reshape(b, s, DIFF_KV_HEADS, 2 * DIFF_DH))
            dv_p.append(vp.reshape(b, s, DIFF_KV_HEADS, 2 * DIFF_DH))
            dk_s.append(ks.reshape(db, t, DIFF_KV_HEADS, 2 * DIFF_DH))
            dv_s.append(vs.reshape(db, t, DIFF_KV_HEADS, 2 * DIFF_DH))
            w_out = diff_w_out[j].astype(BF16)
            yp = _outproj_ln(o_p, yp, mod_p[2], w_out, ln_g[i, 0], ln_b[i, 0], alpha)
            ys = _outproj_ln(o_s, ys, mod_s[2], w_out, ln_g[i, 0], ln_b[i, 0], alpha)
        elif kind == 1:
            hw = DIL_HEADS * DIL_DH
            splits, dts_p, dts_s = [], [], []
            for g in range(len(DIL_PATTERNS)):
                splits += [(3 * g * hw, (3 * g + 1) * hw, 1.0), ((3 * g + 1) * hw, (3 * g + 3) * hw, 1.0)]
                dts_p += [BF16, F32]
                dts_s += [F32, F32]
            w_in = dil_w_in[j].astype(BF16)
            pp = _proj(yp, mod_p[0], mod_p[1], w_in, splits, dts_p)
            ps = _proj(ys, mod_s[0], mod_s[1], w_in, splits, dts_s)
            outs, lses = [], []
            for g, (win, dil) in enumerate(DIL_PATTERNS):
                o_g, lse_g = _dil_prompt_group(pp[2 * g], pp[2 * g + 1], bias_d, win, dil, b, s)
                outs.append(o_g)
                lses.append(lse_g)
                kv = pp[2 * g + 1].reshape(b, s, 2, DIL_HEADS, DIL_DH)
                dl_p[g].append(kv[:, s - min(win, s):])
            bufs = [cb[j] for cb in dil_cache]
            kv_news = [ps[2 * g + 1] for g in range(len(DIL_PATTERNS))]
            o_s = _dil_sample_attn([ps[2 * g] for g in range(len(DIL_PATTERNS))], kv_news, bufs, bias_d, past, db, t)
            for g in range(len(DIL_PATTERNS)):
                new = kv_news[g].reshape(db, t, 2, DIL_HEADS, DIL_DH).astype(bufs[g].dtype)
                dl_s[g].append(jnp.concatenate([bufs[g], new], 1)[:, -bufs[g].shape[1]:])
            w_out = dil_w_out[j].astype(BF16)
            yp = _merge_outproj_ln(outs, lses, yp, mod_p[2], w_out, ln_g[i, 0], ln_b[i, 0], alpha)
            ys = _outproj_ln(o_s, ys, mod_s[2], w_out, ln_g[i, 0], ln_b[i, 0], alpha)
        else:
            win, wqa, wqb, wk, wv, wabs = _mla_weights(mla_w_in[j], mla_w_uq[j], mla_w_uk[j], mla_w_uv[j])
            ckv_p, krp, qp, kfp, vtp = _mla_proj(yp, mod_p[0], mod_p[1], jnp.arange(s), win, mla_q_norm[j],
                                                 mla_kv_norm[j], wqa, wqb, wk, wv, True,
                                                 (MLA_NOPE + MLA_ROPE) ** -0.5 * LOG2E)
            ckv_s, krs, qs = _mla_proj(ys, mod_s[0], mod_s[1], past + jnp.arange(t), win, mla_q_norm[j],
                                       mla_kv_norm[j], wqa, wqb, wk, wv, False, 1.0)
            o_p = _mla_prompt_attn(qp, kfp, vtp, b, s)
            o_s = _mla_sample_attn(qs, ckv_s, krs, cache_mla_ckv, cache_mla_krope, j, page_table, wabs, wv, db, t)
            ck_p.append(ckv_p.reshape(b, s, MLA_KV_RANK))
            kr_p.append(krp.reshape(b, s, MLA_ROPE))
            ck_s.append(ckv_s.reshape(db, t, MLA_KV_RANK))
            kr_s.append(krs.reshape(db, t, MLA_ROPE))
            w_out = mla_w_out[j].astype(BF16)
            yp = _outproj_ln(o_p, yp, mod_p[2], w_out, ln_g[i, 0], ln_b[i, 0], alpha)
            ys = _outproj_ln(o_s, ys, mod_s[2], w_out, ln_g[i, 0], ln_b[i, 0], alpha)
        yp, ys = _moe_layer(yp, ys, mod_p, mod_s, ln_g[i, 1], ln_b[i, 1], moe_wg[i], moe_bg[i], moe_we[i],
                            moe_be[i], moe_w1, moe_w3, moe_w2, i, alpha)
    return (yp, ys, jnp.stack(dk_p), jnp.stack(dv_p), jnp.stack(dk_s), jnp.stack(dv_s),
            jnp.stack(dl_p[0]), jnp.stack(dl_p[1]), jnp.stack(dl_p[2]),
            jnp.stack(dl_s[0]), jnp.stack(dl_s[1]), jnp.stack(dl_s[2]),
            jnp.stack(ck_p), jnp.stack(kr_p), jnp.stack(ck_s), jnp.stack(kr_s))
```

```python
import functools
import math

import jax
import jax.numpy as jnp
from jax import lax
from jax.experimental import pallas as pl
from jax.experimental.pallas import tpu as pltpu

F32 = jnp.float32
BF16 = jnp.bfloat16

N_MIXERS = 3
N_BUCKETS = 32
REL_MAX_DIST = 2048
DIFF_DH = 64
DIFF_HEADS = 8
DIFF_KV_HEADS = 2
DIL_PATTERNS = ((128, 1), (512, 4), (2048, 16))
DIL_HEADS = 8
DIL_DH = 64
MLA_HEADS = 16
MLA_NOPE = 64
MLA_ROPE = 32
MLA_V = 64
MLA_Q_RANK = 384
MLA_KV_RANK = 256
ROPE_BASE = 10000.0
MOE_GROUPS = 4
MOE_EXPERTS_PER_GROUP = 8
MOE_EXPERTS = MOE_GROUPS * MOE_EXPERTS_PER_GROUP
MOE_TOPK = 2
LN_EPS = 1e-5
NEG_INF = -1e30
LOG2E = math.log2(math.e)

LANES = 128
ROW_TILE = 512
MOE_ROWS = 256
DIFF_TILE = 256
DIFF_KEY_TILES = 2
MLA_TILE = 512
MLA_HEADS_PER_STEP = 8
DIL_TILE = 128
PAGES_PER_STEP = 32
VMEM_LIMIT_BYTES = 56 * 1024 * 1024


def _params(*sem):
    return pltpu.CompilerParams(dimension_semantics=sem, vmem_limit_bytes=VMEM_LIMIT_BYTES)


def _dot(a, b):
    return jnp.dot(a, b, preferred_element_type=F32)


def _dot_nt(a, b):
    return lax.dot_general(a, b, (((1,), (1,)), ((), ())), preferred_element_type=F32)


def _resident(shape):
    nd = len(shape)
    return pl.BlockSpec(shape, lambda *_: (0,) * nd)


def _lane(shape, axis):
    return lax.broadcasted_iota(jnp.int32, shape, axis)


def _rel_bucket(dist):
    max_exact = N_BUCKETS // 2
    n = jnp.maximum(dist, 0)
    nf = jnp.maximum(n, 1).astype(F32)
    large = max_exact + (jnp.log(nf / max_exact) / math.log(REL_MAX_DIST / max_exact)
                         * (N_BUCKETS - max_exact)).astype(jnp.int32)
    return jnp.where(n < max_exact, n, jnp.minimum(large, N_BUCKETS - 1))


def _bias_by_dist(rel_bias, n):
    hit = _rel_bucket(jnp.arange(n))[None, :, None] == jnp.arange(N_BUCKETS)[None, None, :]
    return jnp.sum(jnp.where(hit, rel_bias.astype(F32).T[:, None, :], 0.0), -1)


def _toeplitz(w, rows, cols):
    p = rows + cols
    u = jnp.concatenate([w[..., :cols][..., ::-1], w[..., :1], w[..., cols:][..., ::-1]], -1)
    t = jnp.tile(u, (1,) * (w.ndim - 1) + (rows,))[..., :rows * (p - 1)]
    return t.reshape(w.shape[:-1] + (rows, p - 1))[..., :cols]


def _neg(shape):
    return jnp.full(shape, NEG_INF, F32)


def _rope_tables(pos):
    half = MLA_ROPE // 2
    inv = ROPE_BASE ** (-jnp.arange(half, dtype=F32) / half)
    ang = pos.astype(F32)[:, None] * inv[None, :]
    cos, sin = jnp.cos(ang), jnp.sin(ang)
    n = pos.shape[0]
    cq = jnp.concatenate([jnp.ones((n, MLA_NOPE), F32), cos, cos, jnp.zeros((n, 32), F32)], 1)
    sq = jnp.concatenate([jnp.zeros((n, MLA_NOPE), F32), -sin, sin, jnp.zeros((n, 32), F32)], 1)
    ck = jnp.concatenate([cos, cos, jnp.zeros((n, LANES - MLA_ROPE), F32)], 1)
    sk = jnp.concatenate([-sin, sin, jnp.zeros((n, LANES - MLA_ROPE), F32)], 1)
    return cq[None], sq[None], ck[None], sk[None]


def _ada_kernel(c_ref, w_ref, b_ref, o_ref):
    c = c_ref[...]
    a = (c * jax.nn.sigmoid(c)).astype(BF16)
    o_ref[...] = _dot(a, w_ref[...].astype(BF16)) + b_ref[...]


def _ada(c_all, ada_w, ada_b):
    depth, d, n = ada_w.shape
    m = c_all.shape[0]
    tn = 1536
    return pl.pallas_call(
        _ada_kernel,
        grid=(depth, n // tn),
        in_specs=[pl.BlockSpec((m, d), lambda l, j: (0, 0)),
                  pl.BlockSpec((None, d, tn), lambda l, j: (l, 0, j)),
                  pl.BlockSpec((None, 1, tn), lambda l, j: (l, 0, j))],
        out_specs=pl.BlockSpec((None, m, tn), lambda l, j: (l, 0, j)),
        out_shape=jax.ShapeDtypeStruct((depth, m, n), F32),
        compiler_params=_params("arbitrary", "arbitrary"),
        name="ada",
    )(c_all, ada_w, ada_b.reshape(depth, 1, n))


def _row_blocks(b, s):
    if s >= ROW_TILE:
        assert s % ROW_TILE == 0
        return 1, ROW_TILE
    tb = min(b, max(1, ROW_TILE // s))
    assert b % tb == 0
    return tb, s


def _x_spec(tb, ts, s, d):
    nst = s // ts
    return pl.BlockSpec((tb, ts, d), lambda i: (i // nst, i % nst, 0))


def _mod_spec(tb, ts, s, d):
    nst = s // ts
    return pl.BlockSpec((tb, 1, d), lambda i: (i // nst, 0, 0))


def _pos_spec(ts, s, c):
    nst = s // ts
    return pl.BlockSpec((1, ts, c), lambda i: (0, i % nst, 0))


def _rows_spec(rows, n, off=0):
    return pl.BlockSpec((rows, n), lambda i: (i + off, 0))


def _proj_kernel(x_ref, sh_ref, sc_ref, w_ref, *rest, splits, transposed):
    wt_ref = rest[0] if transposed else None
    o_refs = rest[1:] if transposed else rest
    tb, ts, d = x_ref.shape
    h = x_ref[...] * (1.0 + sc_ref[...]) + sh_ref[...]
    h = h.reshape(tb * ts, d).astype(BF16)
    for o_ref, (c0, c1, mul) in zip(o_refs, splits):
        for s in range(c0, c1, 512):
            e = min(s + 512, c1)
            o_ref[:, s - c0:e - c0] = (_dot(h, w_ref[:, s:e]) * mul).astype(o_ref.dtype)
    if transposed:
        o_refs[-1][...] = _dot_nt(wt_ref[...], h).astype(o_refs[-1].dtype)


def _proj(x, shift, scale, w, splits, dtypes, wt=None):
    b, s, d = x.shape
    tb, ts = _row_blocks(b, s)
    rows = tb * ts
    nst = s // ts
    in_specs = [_x_spec(tb, ts, s, d), _mod_spec(tb, ts, s, d), _mod_spec(tb, ts, s, d), _resident(w.shape)]
    out_specs = [_rows_spec(rows, c1 - c0) for (c0, c1, _) in splits]
    out_shape = [jax.ShapeDtypeStruct((b * s, c1 - c0), dt) for (c0, c1, _), dt in zip(splits, dtypes)]
    args = [x, shift, scale, w]
    if wt is not None:
        assert tb == 1
        in_specs.append(_resident(wt.shape))
        args.append(wt)
        out_specs.append(pl.BlockSpec((None, wt.shape[0], ts), lambda i: (i // nst, 0, i % nst)))
        out_shape.append(jax.ShapeDtypeStruct((b, wt.shape[0], s), BF16))
    return pl.pallas_call(
        functools.partial(_proj_kernel, splits=tuple(splits), transposed=wt is not None),
        grid=(b * s // rows,),
        in_specs=in_specs, out_specs=out_specs, out_shape=out_shape,
        compiler_params=_params("arbitrary"),
        name="proj",
    )(*args)


def _layernorm(z, g, b):
    mu = jnp.mean(z, -1, keepdims=True)
    zc = z - mu
    var = jnp.mean(zc * zc, -1, keepdims=True)
    return zc * lax.rsqrt(var + LN_EPS) * g + b


def _outproj_ln_kernel(o_ref, x_ref, gate_ref, w_ref, g_ref, b_ref, y_ref, *, alpha):
    tb, ts, d = x_ref.shape
    f = _dot(o_ref[...].astype(BF16), w_ref[...]).reshape(tb, ts, d)
    z = alpha * x_ref[...] + gate_ref[...] * f
    y_ref[...] = _layernorm(z, g_ref[...], b_ref[...])


def _outproj_ln(o2d, x, gate, w, g, b, alpha):
    bb, s, d = x.shape
    tb, ts = _row_blocks(bb, s)
    rows = tb * ts
    return pl.pallas_call(
        functools.partial(_outproj_ln_kernel, alpha=alpha),
        grid=(bb * s // rows,),
        in_specs=[_rows_spec(rows, o2d.shape[1]), _x_spec(tb, ts, s, d), _mod_spec(tb, ts, s, d),
                  _resident(w.shape), _resident((1, 1, d)), _resident((1, 1, d))],
        out_specs=_x_spec(tb, ts, s, d),
        out_shape=jax.ShapeDtypeStruct(x.shape, F32),
        compiler_params=_params("arbitrary"),
        name="outproj_ln",
    )(o2d, x, gate, w, g.reshape(1, 1, d), b.reshape(1, 1, d))


def _merge_outproj_ln_kernel(o0, o1, o2, l0, l1, l2, x_ref, gate_ref, w_ref, g_ref, b_ref, y_ref, *, alpha):
    tb, ts, d = x_ref.shape
    a0, a1, a2 = l0[...], l1[...], l2[...]
    mx = jnp.maximum(jnp.maximum(a0, a1), a2)
    e0, e1, e2 = jnp.exp(a0 - mx), jnp.exp(a1 - mx), jnp.exp(a2 - mx)
    o = (e0 * o0[...] + e1 * o1[...] + e2 * o2[...]) / (e0 + e1 + e2)
    f = _dot(o.astype(BF16), w_ref[...]).reshape(tb, ts, d)
    z = alpha * x_ref[...] + gate_ref[...] * f
    y_ref[...] = _layernorm(z, g_ref[...], b_ref[...])


def _merge_outproj_ln(outs, lses, x, gate, w, g, b, alpha):
    bb, s, d = x.shape
    tb, ts = _row_blocks(bb, s)
    rows = tb * ts
    k = outs[0].shape[1]
    return pl.pallas_call(
        functools.partial(_merge_outproj_ln_kernel, alpha=alpha),
        grid=(bb * s // rows,),
        in_specs=[_rows_spec(rows, k)] * 6 + [_x_spec(tb, ts, s, d), _mod_spec(tb, ts, s, d),
                                              _resident(w.shape), _resident((1, 1, d)), _resident((1, 1, d))],
        out_specs=_x_spec(tb, ts, s, d),
        out_shape=jax.ShapeDtypeStruct(x.shape, F32),
        compiler_params=_params("arbitrary"),
        name="merge_outproj_ln",
    )(*outs, *lses, x, gate, w, g.reshape(1, 1, d), b.reshape(1, 1, d))


def _softmax_step(s_blocks, vt_blocks, m_sc, l_sc, acc_sc, i):
    m_prev = m_sc[i]
    m_new = m_prev
    for s in s_blocks:
        m_new = jnp.maximum(m_new, jnp.max(s, 0, keepdims=True))
    a = jnp.exp2(m_prev - m_new)
    l = a * l_sc[i]
    acc = a * acc_sc[i]
    for s, v_t in zip(s_blocks, vt_blocks):
        p = jnp.exp2(s - m_new)
        l = l + jnp.sum(p, 0, keepdims=True)
        acc = acc + _dot(v_t, p.astype(BF16))
    l_sc[i] = l
    acc_sc[i] = acc
    m_sc[i] = m_new


def _causal_pairs(nt, kt=1):
    pairs = [(qi, ki) for qi in range(nt) for ki in range(qi // kt + 1)]
    return (jnp.asarray([p[0] for p in pairs], jnp.int32), jnp.asarray([p[1] for p in pairs], jnp.int32))


def _init_stats(m_sc, l_sc, acc_sc):
    m_sc[...] = jnp.full(m_sc.shape, NEG_INF, F32)
    l_sc[...] = jnp.zeros(l_sc.shape, F32)
    acc_sc[...] = jnp.zeros(acc_sc.shape, F32)


def _diff_prompt_kernel(qi_ref, ki_ref, lam_ref, q_ref, k_ref, vt_ref, bias_ref, subln_ref, o_ref, m_sc, l_sc, acc_sc,
                        *, post):
    pair = pl.program_id(2)
    qi, ki = qi_ref[pair], ki_ref[pair]
    r_heads = q_ref.shape[1] // LANES
    t = q_ref.shape[0]
    kt = k_ref.shape[0] // t

    @pl.when(ki == 0)
    def _():
        _init_stats(m_sc, l_sc, acc_sc)

    k = k_ref[...]
    v_ts = [vt_ref[:, j * t:(j + 1) * t] for j in range(kt)]
    first = _lane(k.shape, 1) < DIFF_DH
    rows = k.shape[0]
    k2 = jnp.concatenate([jnp.where(first, k, jnp.zeros_like(k)), jnp.where(first, jnp.zeros_like(k), k)], 0)
    scores = [_dot_nt(k2, q_ref[:, r * LANES:(r + 1) * LANES]) for r in range(r_heads)]
    for r in range(r_heads):
        biases = [bias_ref[r, jnp.maximum(qi - (ki * kt + j) + 1, 0)] for j in range(kt)]
        for mi in range(2):
            blocks = [scores[r][mi * rows + j * t:mi * rows + (j + 1) * t] + biases[j] for j in range(kt)]
            _softmax_step(blocks, v_ts, m_sc, l_sc, acc_sc, 2 * r + mi)

    @pl.when(ki == qi // kt)
    def _():
        lam = lam_ref[0, 0]
        for r in range(r_heads):
            o = acc_sc[2 * r] * (1.0 / l_sc[2 * r]) - lam * (acc_sc[2 * r + 1] * (1.0 / l_sc[2 * r + 1]))
            o = o * lax.rsqrt(jnp.mean(o * o, 0, keepdims=True) + LN_EPS) * subln_ref[...] * post
            o_ref[:, r * LANES:(r + 1) * LANES] = o.T.astype(o_ref.dtype)


def _diff_prompt_attn(q2d, kb2d, v_t, bias_d, lam, lam_init, subln, b, s):
    t = min(DIFF_TILE, s)
    nt = s // t
    r = DIFF_HEADS // DIFF_KV_HEADS
    gx = jnp.concatenate([_neg((DIFF_HEADS, t)), bias_d[:, :nt * t]], 1)
    win = jnp.stack([gx[:, d * t + 1:d * t + 2 * t] for d in range(nt)], 1)[..., ::-1]
    tiles = _toeplitz(win, t, t) * LOG2E
    tiles = jnp.concatenate([_neg((DIFF_HEADS, 1, t, t)), tiles], 1)
    hw = r * LANES
    kt = max(c for c in (1, 2, DIFF_KEY_TILES) if nt % c == 0)
    qis, kis = _causal_pairs(nt, kt)
    grid_spec = pltpu.PrefetchScalarGridSpec(
        num_scalar_prefetch=2,
        grid=(DIFF_KV_HEADS, b, qis.shape[0]),
        in_specs=[pl.BlockSpec(memory_space=pltpu.SMEM),
                  pl.BlockSpec((None, t, hw), lambda g, bb, p, qa, ka: (bb, qa[p], g)),
                  pl.BlockSpec((None, kt * t, LANES), lambda g, bb, p, qa, ka: (bb, ka[p], g)),
                  pl.BlockSpec((None, LANES, kt * t), lambda g, bb, p, qa, ka: (bb, g, ka[p])),
                  pl.BlockSpec((r, nt + 1, t, t), lambda g, bb, p, qa, ka: (g, 0, 0, 0)),
                  pl.BlockSpec((LANES, 1), lambda g, bb, p, qa, ka: (0, 0))],
        out_specs=pl.BlockSpec((None, t, hw), lambda g, bb, p, qa, ka: (bb, qa[p], g)),
        scratch_shapes=[pltpu.VMEM((2 * r, 1, t), F32), pltpu.VMEM((2 * r, 1, t), F32),
                        pltpu.VMEM((2 * r, LANES, t), F32)])
    return pl.pallas_call(
        functools.partial(_diff_prompt_kernel, post=1.0 - lam_init),
        grid_spec=grid_spec,
        out_shape=jax.ShapeDtypeStruct((b, s, DIFF_HEADS * LANES), BF16),
        compiler_params=_params("arbitrary", "arbitrary", "arbitrary"),
        name="diff_prompt_attn",
    )(qis, kis, lam.reshape(1, 1), q2d.reshape(b, s, -1), kb2d.reshape(b, s, -1), v_t, tiles,
      subln.reshape(LANES, 1)).reshape(b * s, -1)


def _as_column(row):
    n = row.shape[1]
    return jnp.sum(jnp.where(_lane((n, n), 0) == _lane((n, n), 1), row, 0.0), 1, keepdims=True)


def _decode_update(scores, vals, m_sc, l_sc, acc_sc):
    m_prev = m_sc[...]
    m_new = m_prev
    for s in scores:
        m_new = jnp.maximum(m_new, jnp.max(s, 0, keepdims=True))
    a = jnp.exp2(m_prev - m_new)
    l = a * l_sc[...]
    pv = None
    for s, v in zip(scores, vals):
        p = jnp.exp2(s - m_new)
        l = l + jnp.sum(p, 0, keepdims=True)
        part = _dot(p.T.astype(BF16), v)
        pv = part if pv is None else pv + part
    acc_sc[...] = _as_column(a) * acc_sc[...] + pv
    l_sc[...] = l
    m_sc[...] = m_new


def _diff_sample_kernel(pt_ref, lam_ref, q_ref, kn_ref, vn_ref, bias_ref, subln_ref, *rest, npg, qmul, post):
    k_refs, v_refs = rest[:npg], rest[npg:2 * npg]
    o_ref, qbd, m_sc, l_sc, acc_sc = rest[2 * npg:]
    c = pl.program_id(1)
    nc = pl.num_programs(1)
    t = q_ref.shape[0]

    @pl.when(c == 0)
    def _():
        q = q_ref[...] * qmul
        first = _lane((t, LANES), 1) < DIFF_DH
        zero = jnp.zeros((t, LANES), F32)
        heads = [q[:, h * LANES:(h + 1) * LANES] for h in range(DIFF_HEADS)]
        pieces = [jnp.where(first, qh, zero) for qh in heads] + [jnp.where(first, zero, qh) for qh in heads]
        qbd[...] = jnp.concatenate(pieces, 0).astype(BF16)
        _init_stats(m_sc, l_sc, acc_sc)

    def update(pages):
        scores = [_dot_nt(kp.astype(BF16), qbd[...]) + bias for kp, _, bias in pages]
        _decode_update(scores, [vp.astype(BF16) for _, vp, _ in pages], m_sc, l_sc, acc_sc)

    update([(k_refs[i][...], v_refs[i][...], bias_ref[c * npg + i]) for i in range(npg)])

    @pl.when(c == nc - 1)
    def _():
        pad = jnp.zeros((k_refs[0].shape[0] - kn_ref.shape[0], LANES), F32)
        update([(jnp.concatenate([kn_ref[...], pad], 0), jnp.concatenate([vn_ref[...], pad], 0),
                 bias_ref[nc * npg])])
        o = acc_sc[...] * _as_column(1.0 / l_sc[...])
        half = o.shape[0] // 2
        o = o[:half] - lam_ref[0, 0] * o[half:]
        o_ref[...] = o * lax.rsqrt(jnp.mean(o * o, -1, keepdims=True) + LN_EPS) * subln_ref[...] * post


def _diff_sample_attn(q2d, k2d, v2d, cache_k, cache_v, slot, page_table, bias_d, lam, lam_init, subln, db, t):
    n_phys, page = cache_k.shape[1], cache_k.shape[2]
    n_pages = page_table.shape[1]
    past = n_pages * page
    npg = min(PAGES_PER_STEP, n_pages)
    assert n_pages % npg == 0
    prow = page * DIFF_KV_HEADS
    ck = cache_k.reshape(cache_k.shape[0], n_phys, prow, LANES)
    cv = cache_v.reshape(cache_v.shape[0], n_phys, prow, LANES)
    cols = 2 * DIFF_HEADS * t
    assert cols == LANES
    npos = (n_pages + 1) * page
    f = _toeplitz(jnp.concatenate([_neg((DIFF_HEADS, page - 1)), bias_d[:, :past + t]], 1), t, npos)
    f = f.reshape(DIFF_HEADS, t, n_pages + 1, page).transpose(2, 3, 0, 1)
    own = (jnp.arange(DIFF_KV_HEADS)[:, None] == (jnp.arange(DIFF_HEADS) // (DIFF_HEADS // DIFF_KV_HEADS))[None, :])
    table = jnp.where(own[None, None, :, None, :, None], f[:, :, None, None, :, :] * LOG2E, NEG_INF)
    table = jnp.broadcast_to(table, (n_pages + 1, page, DIFF_KV_HEADS, 2, DIFF_HEADS, t))
    table = table.reshape(n_pages + 1, prow, cols)

    def page_spec(i):
        return pl.BlockSpec((None, None, prow, LANES), lambda b, c, pt: (slot, pt[b, c * npg + i], 0, 0))

    grid_spec = pltpu.PrefetchScalarGridSpec(
        num_scalar_prefetch=1,
        grid=(db, n_pages // npg),
        in_specs=[pl.BlockSpec(memory_space=pltpu.SMEM),
                  pl.BlockSpec((None, t, DIFF_HEADS * LANES), lambda b, c, pt: (b, 0, 0)),
                  pl.BlockSpec((None, t * DIFF_KV_HEADS, LANES), lambda b, c, pt: (b, 0, 0)),
                  pl.BlockSpec((None, t * DIFF_KV_HEADS, LANES), lambda b, c, pt: (b, 0, 0)),
                  pl.BlockSpec(table.shape, lambda b, c, pt: (0, 0, 0)),
                  pl.BlockSpec((1, LANES), lambda b, c, pt: (0, 0))]
        + [page_spec(i) for i in range(npg)] + [page_spec(i) for i in range(npg)],
        out_specs=pl.BlockSpec((None, cols // 2, LANES), lambda b, c, pt: (b, 0, 0)),
        scratch_shapes=[pltpu.VMEM((cols, LANES), BF16),
                        pltpu.VMEM((1, cols), F32), pltpu.VMEM((1, cols), F32), pltpu.VMEM((cols, LANES), F32)])
    o = pl.pallas_call(
        functools.partial(_diff_sample_kernel, npg=npg, qmul=DIFF_DH ** -0.5 * LOG2E, post=1.0 - lam_init),
        grid_spec=grid_spec,
        out_shape=jax.ShapeDtypeStruct((db, cols // 2, LANES), F32),
        compiler_params=_params("arbitrary", "arbitrary"),
        name="diff_sample_attn",
    )(page_table, lam.reshape(1, 1), q2d.reshape(db, t, -1), k2d.reshape(db, t * DIFF_KV_HEADS, LANES),
      v2d.reshape(db, t * DIFF_KV_HEADS, LANES), table, subln.reshape(1, LANES),
      *([ck] * npg), *([cv] * npg))
    return o.reshape(db, DIFF_HEADS, t, LANES).transpose(0, 2, 1, 3).reshape(db * t, DIFF_HEADS * LANES)


def _dil_prompt_kernel(q_ref, kp_ref, kc_ref, vp_ref, vc_ref, bias_ref, o_ref, lse_ref, *, scale):
    qi = pl.program_id(2)
    tq = q_ref.shape[0]
    q = q_ref[...]
    k = jnp.concatenate([kp_ref[...], kc_ref[...]], 0).astype(BF16)
    v = jnp.concatenate([vp_ref[...], vc_ref[...]], 0).astype(BF16)
    col = _lane((tq, 2 * tq), 1)
    prev_ok = (col >= tq) | (qi > 0)
    first = _lane((tq, LANES), 1) < DIL_DH
    for p in range(DIL_HEADS // 2):
        sl = slice(p * LANES, (p + 1) * LANES)
        qp, kp, vp = q[:, sl], k[:, sl], v[:, sl]
        outs, lses = [], []
        for a in range(2):
            qa = jnp.where(first, qp, jnp.zeros_like(qp)) if a == 0 else jnp.where(first, jnp.zeros_like(qp), qp)
            s = _dot_nt(qa, kp) * scale + bias_ref[2 * p + a]
            s = jnp.where(prev_ok, s, NEG_INF)
            m = jnp.max(s, -1, keepdims=True)
            e = jnp.exp(s - m)
            l = jnp.sum(e, -1, keepdims=True)
            outs.append(_dot(e.astype(BF16), vp) / l)
            lses.append(m + jnp.log(l))
        o_ref[:, sl] = jnp.where(first, outs[0], outs[1])
        lse_ref[:, sl] = jnp.where(first, jnp.broadcast_to(lses[0], (tq, LANES)),
                                   jnp.broadcast_to(lses[1], (tq, LANES)))


def _dil_prompt_group(q2d, kv2d, bias_d, window, dil, b, s):
    ln = s // dil
    band = window // dil
    tq = min(DIL_TILE, ln)
    assert band <= tq or ln == tq
    nb = ln // tq
    hw = DIL_HEADS * DIL_DH
    wvec = jnp.concatenate([_neg((DIL_HEADS, tq - 1)), bias_d[:, :band * dil + 1:dil],
                            _neg((DIL_HEADS, 2 * tq - band - 1))], 1)
    table = _toeplitz(wvec, tq, 2 * tq)
    qv = q2d.reshape(b, ln, dil * hw)
    kvv = kv2d.reshape(b, ln, dil * 2 * hw)

    def kv_spec(prev, val):
        return pl.BlockSpec((None, tq, hw),
                            lambda bb, c, qi: (bb, jnp.maximum(qi - 1, 0) if prev else qi, 2 * c + val))

    o, lse = pl.pallas_call(
        functools.partial(_dil_prompt_kernel, scale=DIL_DH ** -0.5),
        grid=(b, dil, nb),
        in_specs=[pl.BlockSpec((None, tq, hw), lambda bb, c, qi: (bb, qi, c)),
                  kv_spec(True, 0), kv_spec(False, 0), kv_spec(True, 1), kv_spec(False, 1),
                  pl.BlockSpec(table.shape, lambda bb, c, qi: (0, 0, 0))],
        out_specs=[pl.BlockSpec((None, tq, hw), lambda bb, c, qi: (bb, qi, c))] * 2,
        out_shape=[jax.ShapeDtypeStruct((b, ln, dil * hw), F32)] * 2,
        compiler_params=_params("arbitrary", "arbitrary", "arbitrary"),
        name="dil_prompt_attn",
    )(qv, kvv, kvv, kvv, kvv, table)
    return o.reshape(b * s, hw), lse.reshape(b * s, hw)


def _dil_sample_kernel(q0, q1, q2, kv0, kv1, kv2, b0, b1, b2, o_ref, *, scale):
    t = q0.shape[0]
    hw = DIL_HEADS * DIL_DH
    rows = DIL_HEADS * t
    diag = (_lane((rows, hw), 1) // DIL_DH) == (_lane((rows, hw), 0) // t)
    outs, lses = [], []
    for q_ref, kv_ref, b_ref in ((q0, kv0, b0), (q1, kv1, b1), (q2, kv2, b2)):
        q = q_ref[...]
        qbd = jnp.where(diag, jnp.concatenate([q] * DIL_HEADS, 0), 0.0).astype(BF16)
        s = _dot_nt(qbd, kv_ref[:, :hw]) * scale + b_ref[...]
        m = jnp.max(s, -1, keepdims=True)
        e = jnp.exp(s - m)
        l = jnp.sum(e, -1, keepdims=True)
        outs.append(_dot(e.astype(BF16), kv_ref[:, hw:]) / l)
        lses.append(m + jnp.log(l))
    mx = jnp.maximum(jnp.maximum(lses[0], lses[1]), lses[2])
    es = [jnp.exp(x - mx) for x in lses]
    o = (es[0] * outs[0] + es[1] * outs[1] + es[2] * outs[2]) / (es[0] + es[1] + es[2])
    o = jnp.where(diag, o, 0.0)
    o_ref[...] = jnp.sum(o.reshape(DIL_HEADS, t, hw), 0)


def _dil_sample_attn(qs, kv_news, bufs, bias_d, past, db, t):
    hw = DIL_HEADS * DIL_DH
    rows = DIL_HEADS * t
    kvs, tables = [], []
    for (window, dil), kvn, buf in zip(DIL_PATTERNS, kv_news, bufs):
        wb = buf.shape[1]
        pad = -(wb + t) % LANES
        kvs.append(jnp.concatenate([buf.reshape(db, wb, 2 * hw).astype(BF16),
                                    kvn.reshape(db, t, 2 * hw).astype(BF16),
                                    jnp.zeros((db, pad, 2 * hw), BF16)], 1))
        wp = wb + t + pad
        f = _toeplitz(jnp.concatenate([_neg((DIL_HEADS, wp - 1 - wb)), bias_d[:, :wb + t]], 1), t, wp)
        delta = wb + jnp.arange(t)[:, None] - jnp.arange(wp)[None, :]
        ok = (delta >= 0) & (delta % dil == 0) & (delta <= window) & (past - delta + jnp.arange(t)[:, None] >= 0)
        tables.append(jnp.where(ok[None], f, NEG_INF).reshape(rows, wp))
    o = pl.pallas_call(
        functools.partial(_dil_sample_kernel, scale=DIL_DH ** -0.5),
        grid=(db,),
        in_specs=[pl.BlockSpec((None, t, hw), lambda b: (b, 0, 0))] * 3
        + [pl.BlockSpec((None,) + kv.shape[1:], lambda b: (b, 0, 0)) for kv in kvs]
        + [_resident(tb.shape) for tb in tables],
        out_specs=pl.BlockSpec((None, t, hw), lambda b: (b, 0, 0)),
        out_shape=jax.ShapeDtypeStruct((db, t, hw), F32),
        compiler_params=_params("arbitrary"),
        name="dil_sample_attn",
    )(*[q.reshape(db, t, hw) for q in qs], *kvs, *tables)
    return o.reshape(db * t, hw)


def _mla_proj_kernel(x_ref, sh_ref, sc_ref, cq_ref, sq_ref, ck_ref, sk_ref, win_ref, qn_ref, kvn_ref,
                     wqa_ref, wqb_ref, *rest, with_kv, qmul):
    if with_kv:
        wk_ref, wv_ref, ckv_ref, kr_ref, q_ref, k_ref, v_ref = rest
    else:
        ckv_ref, kr_ref, q_ref = rest
    tb, ts, d = x_ref.shape
    rows = tb * ts
    h = x_ref[...] * (1.0 + sc_ref[...]) + sh_ref[...]
    h = h.reshape(rows, d).astype(BF16)
    proj = _dot(h, win_ref[...])
    r0, r1 = MLA_Q_RANK, MLA_Q_RANK + MLA_KV_RANK

    def rms(z, g):
        return z * lax.rsqrt(jnp.mean(z * z, -1, keepdims=True) + LN_EPS) * g

    cq = rms(proj[:, :r0], qn_ref[...]).astype(BF16)
    ckv = rms(proj[:, r0:r1], kvn_ref[...])
    kra = proj[:, r1:r1 + LANES].reshape(tb, ts, LANES)
    krb = proj[:, r1 + LANES:r1 + 2 * LANES].reshape(tb, ts, LANES)
    kr = (kra * ck_ref[...] + krb * sk_ref[...]).reshape(rows, LANES)
    ckv_ref[...] = ckv
    kr_ref[...] = kr[:, :MLA_ROPE]
    nq = q_ref.shape[1]
    cqt = jnp.concatenate([cq_ref[...]] * MLA_HEADS, -1)
    sqt = jnp.concatenate([sq_ref[...]] * MLA_HEADS, -1)
    qa = _dot(cq, wqa_ref[...]).reshape(tb, ts, nq)
    qb = _dot(cq, wqb_ref[...]).reshape(tb, ts, nq)
    q_ref[...] = ((qa * cqt + qb * sqt) * qmul).reshape(rows, nq).astype(q_ref.dtype)
    if with_kv:
        kcat = jnp.concatenate([ckv.astype(BF16), kr.astype(BF16)], 1)
        k_ref[...] = _dot(kcat, wk_ref[...]).astype(k_ref.dtype)
        v_ref[...] = _dot_nt(wv_ref[...], ckv.astype(BF16)).astype(v_ref.dtype)


def _mla_weights(w_in, w_uq, w_uk, w_uv):
    d = w_in.shape[0]
    r0, r1 = MLA_Q_RANK, MLA_Q_RANK + MLA_KV_RANK
    half = MLA_ROPE // 2
    zpad = jnp.zeros((d, LANES - MLA_ROPE), F32)
    kr_w = w_in[:, r1:]
    kr_sw = jnp.concatenate([kr_w[:, half:], kr_w[:, :half]], 1)
    win = jnp.concatenate([w_in[:, :r1], kr_w, zpad, kr_sw, zpad], 1).astype(BF16)
    hd = MLA_NOPE + MLA_ROPE
    wq = w_uq.reshape(MLA_Q_RANK, MLA_HEADS, hd)
    z32 = jnp.zeros((MLA_Q_RANK, MLA_HEADS, LANES - hd), F32)
    z64 = jnp.zeros((MLA_Q_RANK, MLA_HEADS, MLA_NOPE), F32)
    wqa = jnp.concatenate([wq, z32], -1).reshape(MLA_Q_RANK, MLA_HEADS * LANES).astype(BF16)
    wqb = jnp.concatenate([z64, wq[..., MLA_NOPE + half:], wq[..., MLA_NOPE:MLA_NOPE + half], z32], -1)
    wqb = wqb.reshape(MLA_Q_RANK, MLA_HEADS * LANES).astype(BF16)
    top = jnp.concatenate([w_uk, jnp.zeros((MLA_KV_RANK, MLA_HEADS, LANES - MLA_NOPE), F32)], -1)
    eye = jnp.eye(MLA_ROPE, dtype=F32)[:, None, :]
    mid = jnp.concatenate([jnp.zeros((MLA_ROPE, MLA_HEADS, MLA_NOPE), F32),
                           jnp.broadcast_to(eye, (MLA_ROPE, MLA_HEADS, MLA_ROPE)),
                           jnp.zeros((MLA_ROPE, MLA_HEADS, LANES - hd), F32)], -1)
    bot = jnp.zeros((LANES - MLA_ROPE, MLA_HEADS, LANES), F32)
    wk = jnp.concatenate([top, mid, bot], 0).reshape(MLA_KV_RANK + LANES, MLA_HEADS * LANES).astype(BF16)
    wv = w_uv.reshape(MLA_KV_RANK, MLA_HEADS * MLA_V).astype(BF16)
    a_top = jnp.concatenate([w_uk.transpose(1, 2, 0),
                             jnp.zeros((MLA_HEADS, MLA_NOPE, LANES), F32)], -1)
    a_mid = jnp.concatenate([jnp.zeros((MLA_ROPE, MLA_KV_RANK), F32), jnp.eye(MLA_ROPE, dtype=F32),
                             jnp.zeros((MLA_ROPE, LANES - MLA_ROPE), F32)], -1)
    a_mid = jnp.broadcast_to(a_mid[None], (MLA_HEADS, MLA_ROPE, MLA_KV_RANK + LANES))
    a_bot = jnp.zeros((MLA_HEADS, LANES - hd, MLA_KV_RANK + LANES), F32)
    wabs = jnp.concatenate([a_top, a_mid, a_bot], 1).reshape(MLA_HEADS * LANES, MLA_KV_RANK + LANES).astype(BF16)
    return win, wqa, wqb, wk, wv, wabs


def _mla_proj(x, shift, scale, pos, win, qn, kvn, wqa, wqb, wk, wv, with_kv, qmul):
    b, s, d = x.shape
    tb, ts = _row_blocks(b, s)
    rows = tb * ts
    cq, sq, ck, sk = _rope_tables(pos)
    nq = MLA_HEADS * LANES
    in_specs = [_x_spec(tb, ts, s, d), _mod_spec(tb, ts, s, d), _mod_spec(tb, ts, s, d),
                _pos_spec(ts, s, LANES), _pos_spec(ts, s, LANES), _pos_spec(ts, s, LANES), _pos_spec(ts, s, LANES),
                _resident(win.shape), _resident((1, MLA_Q_RANK)), _resident((1, MLA_KV_RANK)),
                _resident(wqa.shape), _resident(wqb.shape)]
    args = [x, shift, scale, cq, sq, ck, sk, win, qn.reshape(1, -1), kvn.reshape(1, -1), wqa, wqb]
    out_specs = [_rows_spec(rows, MLA_KV_RANK), _rows_spec(rows, MLA_ROPE), _rows_spec(rows, nq)]
    out_shape = [jax.ShapeDtypeStruct((b * s, MLA_KV_RANK), F32), jax.ShapeDtypeStruct((b * s, MLA_ROPE), F32),
                 jax.ShapeDtypeStruct((b * s, nq), BF16 if with_kv else F32)]
    if with_kv:
        assert tb == 1
        nst = s // ts
        nv = MLA_HEADS * MLA_V
        wv_t = wv.T
        in_specs += [_resident(wk.shape), _resident(wv_t.shape)]
        args += [wk, wv_t]
        out_specs += [_rows_spec(rows, nq), pl.BlockSpec((None, nv, ts), lambda i: (i // nst, 0, i % nst))]
        out_shape += [jax.ShapeDtypeStruct((b * s, nq), BF16), jax.ShapeDtypeStruct((b, nv, s), BF16)]
    return pl.pallas_call(
        functools.partial(_mla_proj_kernel, with_kv=with_kv, qmul=qmul),
        grid=(b * s // rows,),
        in_specs=in_specs, out_specs=out_specs, out_shape=out_shape,
        compiler_params=_params("arbitrary"),
        name="mla_proj",
    )(*args)


def _mla_prompt_kernel(qi_ref, ki_ref, q_ref, k_ref, vt_ref, o_ref, m_sc, l_sc, acc_sc):
    pair = pl.program_id(2)
    qi, ki = qi_ref[pair], ki_ref[pair]
    t = q_ref.shape[0]
    n_heads = q_ref.shape[1] // LANES

    @pl.when(ki == 0)
    def _():
        _init_stats(m_sc, l_sc, acc_sc)

    def update(diagonal):
        scores = [_dot_nt(k_ref[:, a * LANES:(a + 1) * LANES], q_ref[:, a * LANES:(a + 1) * LANES])
                  for a in range(n_heads)]
        for a in range(n_heads):
            s = scores[a]
            if diagonal:
                s = jnp.where(_lane((t, t), 0) <= _lane((t, t), 1), s, NEG_INF)
            _softmax_step([s], [vt_ref[(a // 2) * LANES:(a // 2 + 1) * LANES, :]], m_sc, l_sc, acc_sc, a)

    pl.when(ki < qi)(functools.partial(update, False))
    pl.when(ki == qi)(functools.partial(update, True))

    @pl.when(ki == qi)
    def _():
        first = _lane((LANES, t), 0) < MLA_V
        for p in range(n_heads // 2):
            o = jnp.where(first, acc_sc[2 * p] * (1.0 / l_sc[2 * p]), acc_sc[2 * p + 1] * (1.0 / l_sc[2 * p + 1]))
            o_ref[:, p * LANES:(p + 1) * LANES] = o.T.astype(o_ref.dtype)


def _mla_prompt_attn(q2d, k2d, v_t, b, s):
    t = min(MLA_TILE, s)
    nt = s // t
    hs = MLA_HEADS_PER_STEP
    qis, kis = _causal_pairs(nt)
    grid_spec = pltpu.PrefetchScalarGridSpec(
        num_scalar_prefetch=2,
        grid=(b, MLA_HEADS // hs, qis.shape[0]),
        in_specs=[pl.BlockSpec((None, t, hs * LANES), lambda bb, h, p, qa, ka: (bb, qa[p], h)),
                  pl.BlockSpec((None, t, hs * LANES), lambda bb, h, p, qa, ka: (bb, ka[p], h)),
                  pl.BlockSpec((None, hs * MLA_V, t), lambda bb, h, p, qa, ka: (bb, h, ka[p]))],
        out_specs=pl.BlockSpec((None, t, hs * MLA_V), lambda bb, h, p, qa, ka: (bb, qa[p], h)),
        scratch_shapes=[pltpu.VMEM((hs, 1, t), F32), pltpu.VMEM((hs, 1, t), F32), pltpu.VMEM((hs, LANES, t), F32)])
    return pl.pallas_call(
        _mla_prompt_kernel,
        grid_spec=grid_spec,
        out_shape=jax.ShapeDtypeStruct((b, s, MLA_HEADS * MLA_V), BF16),
        compiler_params=_params("arbitrary", "arbitrary", "arbitrary"),
        name="mla_prompt_attn",
    )(qis, kis, q2d.reshape(b, s, -1), k2d.reshape(b, s, -1), v_t).reshape(b * s, -1)


def _mla_sample_kernel(pt_ref, q_ref, cn_ref, rn_ref, wabs_ref, wv_ref, *rest, npg, qmul):
    c_refs, r_refs = rest[:npg], rest[npg:2 * npg]
    o_ref, qabs, rpad, rnew, m_sc, l_sc, acc_sc = rest[2 * npg:]
    c = pl.program_id(1)
    nc = pl.num_programs(1)
    t = q_ref.shape[0]
    cols = MLA_HEADS * t
    page = c_refs[0].shape[0]

    @pl.when(c == 0)
    def _():
        q = q_ref[...]
        qs = jnp.concatenate([q[:, h * LANES:(h + 1) * LANES] for h in range(MLA_HEADS)], 0)
        rh = _lane((cols, LANES), 0) // t
        qbd = jnp.concatenate([jnp.where(rh == h, qs, 0.0) for h in range(MLA_HEADS)], 1).astype(BF16)
        qabs[...] = (_dot(qbd, wabs_ref[...]) * qmul).astype(BF16)
        rpad[...] = jnp.zeros(rpad.shape, F32)
        rnew[...] = jnp.zeros(rnew.shape, F32)
        _init_stats(m_sc, l_sc, acc_sc)

    def update(lats, valid):
        scores, latbs = [], []
        for lat, rot in lats:
            latb = lat.astype(BF16)
            s = _dot_nt(jnp.concatenate([latb, rot.astype(BF16)], 1), qabs[...])
            scores.append(s if valid is None else jnp.where(valid, s, NEG_INF))
            latbs.append(latb)
        _decode_update(scores, latbs, m_sc, l_sc, acc_sc)

    for i in range(npg):
        rpad[i, :MLA_ROPE, :] = r_refs[i][...]
    update([(c_refs[i][...], rpad[i].T) for i in range(npg)], None)

    @pl.when(c == nc - 1)
    def _():
        rnew[:t, :MLA_ROPE] = rn_ref[...]
        lat = jnp.concatenate([cn_ref[...], jnp.zeros((page - t, MLA_KV_RANK), F32)], 0)
        update([(lat, rnew[...])], _lane((page, cols), 0) <= (_lane((page, cols), 1) % t))
        o_lat = (acc_sc[...] * _as_column(1.0 / l_sc[...])).astype(BF16)
        of = _dot(o_lat, wv_ref[...])
        n = of.shape[1]
        own = (_lane((cols, n), 1) // MLA_V) == (_lane((cols, n), 0) // t)
        o_ref[...] = jnp.sum(jnp.where(own, of, 0.0).reshape(MLA_HEADS, t, n), 0)


def _mla_sample_attn(q2d, ckv_new, kr_new, cache_ckv, cache_krope, slot, page_table, wabs, wv, db, t):
    page = cache_ckv.shape[2]
    n_pages = page_table.shape[1]
    npg = min(PAGES_PER_STEP, n_pages)
    assert n_pages % npg == 0
    cols = MLA_HEADS * t
    nv = MLA_HEADS * MLA_V

    assert page == LANES
    krope_t = cache_krope.transpose(0, 1, 3, 2)

    def page_spec(w, i):
        return pl.BlockSpec((None, None, page, w), lambda b, c, pt: (slot, pt[b, c * npg + i], 0, 0))

    def rope_spec(i):
        return pl.BlockSpec((None, None, MLA_ROPE, page), lambda b, c, pt: (slot, pt[b, c * npg + i], 0, 0))

    grid_spec = pltpu.PrefetchScalarGridSpec(
        num_scalar_prefetch=1,
        grid=(db, n_pages // npg),
        in_specs=[pl.BlockSpec((None, t, MLA_HEADS * LANES), lambda b, c, pt: (b, 0, 0)),
                  pl.BlockSpec((None, t, MLA_KV_RANK), lambda b, c, pt: (b, 0, 0)),
                  pl.BlockSpec((None, t, MLA_ROPE), lambda b, c, pt: (b, 0, 0)),
                  pl.BlockSpec(wabs.shape, lambda b, c, pt: (0, 0)),
                  pl.BlockSpec(wv.shape, lambda b, c, pt: (0, 0))]
        + [page_spec(MLA_KV_RANK, i) for i in range(npg)] + [rope_spec(i) for i in range(npg)],
        out_specs=pl.BlockSpec((None, t, nv), lambda b, c, pt: (b, 0, 0)),
        scratch_shapes=[pltpu.VMEM((cols, MLA_KV_RANK + LANES), BF16), pltpu.VMEM((npg, LANES, page), F32),
                        pltpu.VMEM((page, LANES), F32),
                        pltpu.VMEM((1, cols), F32), pltpu.VMEM((1, cols), F32),
                        pltpu.VMEM((cols, MLA_KV_RANK), F32)])
    o = pl.pallas_call(
        functools.partial(_mla_sample_kernel, npg=npg, qmul=(MLA_NOPE + MLA_ROPE) ** -0.5 * LOG2E),
        grid_spec=grid_spec,
        out_shape=jax.ShapeDtypeStruct((db, t, nv), F32),
        compiler_params=_params("arbitrary", "arbitrary"),
        name="mla_sample_attn",
    )(page_table, q2d.reshape(db, t, -1), ckv_new.reshape(db, t, -1), kr_new.reshape(db, t, -1), wabs, wv,
      *([cache_ckv] * npg), *([krope_t] * npg))
    return o.reshape(db * t, nv)


def _router_kernel(x_ref, sh_ref, sc_ref, whi_ref, wlo_ref, b_ref, h_ref, lg_ref):
    tb, ts, d = x_ref.shape
    h = (x_ref[...] * (1.0 + sc_ref[...]) + sh_ref[...]).reshape(tb * ts, d)
    hi = h.astype(BF16)
    lo = (h - hi.astype(F32)).astype(BF16)
    h_ref[...] = hi
    lg = _dot(hi, whi_ref[...]) + _dot(hi, wlo_ref[...]) + _dot(lo, whi_ref[...]) + b_ref[...]
    lane = _lane(lg.shape, 1)
    far = jnp.int32(LANES)

    def top(vals):
        v = jnp.max(vals, -1, keepdims=True)
        return v, jnp.min(jnp.where(vals == v, lane, far), -1, keepdims=True)

    is_group = lane < MOE_GROUPS
    gmax, gsel = top(jnp.where(is_group, lg, NEG_INF))
    gprob = 1.0 / jnp.sum(jnp.where(is_group, jnp.exp(lg - gmax), 0.0), -1, keepdims=True)
    lo_lane = MOE_GROUPS + MOE_EXPERTS_PER_GROUP * gsel
    el = jnp.where((lane >= lo_lane) & (lane < lo_lane + MOE_EXPERTS_PER_GROUP), lg, NEG_INF)
    v1, i1 = top(el)
    v2, i2 = top(jnp.where(lane == i1, NEG_INF, el))
    ex = jnp.exp(v2 - v1)
    g1 = gprob / (1.0 + ex)
    g2 = g1 * ex
    e1 = (i1 - MOE_GROUPS).astype(F32)
    e2 = (i2 - MOE_GROUPS).astype(F32)
    lg_ref[...] = jnp.where(lane == 0, e1, jnp.where(lane == 1, e2, jnp.where(lane == 2, g1, jnp.where(lane == 3, g2, 0.0))))


def _router(x, shift, scale, whi, wlo, bias):
    b, s, d = x.shape
    tb, ts = _row_blocks(b, s)
    rows = tb * ts
    return pl.pallas_call(
        _router_kernel,
        grid=(b * s // rows,),
        in_specs=[_x_spec(tb, ts, s, d), _mod_spec(tb, ts, s, d), _mod_spec(tb, ts, s, d),
                  _resident(whi.shape), _resident(wlo.shape), _resident(bias.shape)],
        out_specs=[_rows_spec(rows, d), _rows_spec(rows, LANES)],
        out_shape=[jax.ShapeDtypeStruct((b * s, d), BF16), jax.ShapeDtypeStruct((b * s, LANES), F32)],
        compiler_params=_params("arbitrary"),
        name="router",
    )(x, shift, scale, whi, wlo, bias)


def _expert_kernel(blk_ref, exp_ref, first_ref, fresh_ref, lo_ref, hi_ref, n_ref, x_ref, w1_ref, w3_ref, w2_ref,
                   y_ref, w1_sc, w3_sc, w2_sc):
    w = pl.program_id(0)

    @pl.when(w < n_ref[0])
    def _():
        @pl.when(fresh_ref[w] == 1)
        def _():
            w1_sc[...] = w1_ref[...].astype(BF16)
            w3_sc[...] = w3_ref[...].astype(BF16)
            w2_sc[...] = w2_ref[...].astype(BF16)

        x = x_ref[...]
        a = _dot(x, w1_sc[...])
        u = (a * jax.nn.sigmoid(a)) * _dot(x, w3_sc[...])
        y = _dot(u.astype(BF16), w2_sc[...]).astype(y_ref.dtype)
        row = blk_ref[w] * x.shape[0] + _lane((x.shape[0], 1), 0)
        mine = (row >= lo_ref[w]) & (row < hi_ref[w])

        @pl.when(first_ref[w] == 1)
        def _():
            y_ref[...] = jnp.where(mine, y, jnp.zeros_like(y))

        @pl.when(first_ref[w] == 0)
        def _():
            y_ref[...] = jnp.where(mine, y, y_ref[...])


def _experts(xs, items, w1, w3, w2, layer):
    a, d = xs.shape
    ff = w1.shape[3]
    n_items = items[0].shape[0]
    grid_spec = pltpu.PrefetchScalarGridSpec(
        num_scalar_prefetch=7,
        grid=(n_items,),
        in_specs=[pl.BlockSpec((MOE_ROWS, d), lambda w, blk, ex, *_: (blk[w], 0)),
                  pl.BlockSpec((None, None, d, ff), lambda w, blk, ex, *_: (layer, ex[w], 0, 0)),
                  pl.BlockSpec((None, None, d, ff), lambda w, blk, ex, *_: (layer, ex[w], 0, 0)),
                  pl.BlockSpec((None, None, ff, d), lambda w, blk, ex, *_: (layer, ex[w], 0, 0))],
        out_specs=pl.BlockSpec((MOE_ROWS, d), lambda w, blk, ex, *_: (blk[w], 0)),
        scratch_shapes=[pltpu.VMEM((d, ff), BF16), pltpu.VMEM((d, ff), BF16), pltpu.VMEM((ff, d), BF16)])
    return pl.pallas_call(
        _expert_kernel,
        grid_spec=grid_spec,
        out_shape=jax.ShapeDtypeStruct((a, d), BF16),
        compiler_params=_params("arbitrary"),
        name="experts",
    )(*items, xs, w1, w3, w2)


def _combine_ln_kernel(y0_ref, y1_ref, r_ref, x_ref, gate_ref, g_ref, b_ref, o_ref, *, alpha):
    tb, ts, d = x_ref.shape
    r = r_ref[...]
    y = y0_ref[...].astype(F32) * r[:, 2:3] + y1_ref[...].astype(F32) * r[:, 3:4]
    z = alpha * x_ref[...] + gate_ref[...] * y.reshape(tb, ts, d)
    o_ref[...] = _layernorm(z, g_ref[...], b_ref[...])


def _combine_ln(y0, y1, route, row_off, x, gate, g, b, alpha):
    bb, s, d = x.shape
    tb, ts = _row_blocks(bb, s)
    rows = tb * ts
    assert row_off % rows == 0
    off = row_off // rows
    return pl.pallas_call(
        functools.partial(_combine_ln_kernel, alpha=alpha),
        grid=(bb * s // rows,),
        in_specs=[_rows_spec(rows, d, off), _rows_spec(rows, d, off), _rows_spec(rows, LANES, off),
                  _x_spec(tb, ts, s, d), _mod_spec(tb, ts, s, d), _resident((1, 1, d)), _resident((1, 1, d))],
        out_specs=_x_spec(tb, ts, s, d),
        out_shape=jax.ShapeDtypeStruct(x.shape, F32),
        compiler_params=_params("arbitrary"),
        name="combine_ln",
    )(y0, y1, route, x, gate, g.reshape(1, 1, d), b.reshape(1, 1, d))


def _dispatch(eid):
    n = eid.shape[0]
    a = n * MOE_TOPK
    n_blocks = -(-a // MOE_ROWS)
    flat_e = eid.reshape(a)
    order = jnp.argsort(flat_e).astype(jnp.int32)
    inv = jnp.argsort(order).astype(jnp.int32).reshape(n, MOE_TOPK)
    counts = jnp.sum((flat_e[:, None] == jnp.arange(MOE_EXPERTS)[None, :]).astype(jnp.int32), 0)
    end = jnp.cumsum(counts)
    start = end - counts
    b0 = start // MOE_ROWS
    nb = jnp.where(counts > 0, (end - 1) // MOE_ROWS - b0 + 1, 0)
    cum = jnp.cumsum(nb)
    n_items = n_blocks + MOE_EXPERTS - 1
    w = jnp.arange(n_items)
    ex = jnp.minimum(jnp.sum((cum[None, :] <= w[:, None]).astype(jnp.int32), 1), MOE_EXPERTS - 1)
    live = w < cum[-1]
    blk = jnp.where(live, b0[ex] + w - (cum - nb)[ex], n_blocks - 1)
    ex = jnp.where(live, ex, ex[jnp.maximum(cum[-1] - 1, 0)])
    prev_blk = jnp.concatenate([jnp.full((1,), -1, blk.dtype), blk[:-1]])
    prev_ex = jnp.concatenate([jnp.full((1,), -1, ex.dtype), ex[:-1]])
    items = [blk, ex, blk != prev_blk, ex != prev_ex, start[ex], end[ex], cum[-1:]]
    return order, inv, [x.astype(jnp.int32) for x in items]


def _moe_layer(yp, ys, mod_p, mod_s, g, b, wg, bg, we, be, w1, w3, w2, layer, alpha):
    d = yp.shape[-1]
    npad = LANES - MOE_GROUPS - MOE_EXPERTS
    wr = jnp.concatenate([wg, we, jnp.zeros((d, npad), F32)], 1)
    whi = wr.astype(BF16)
    wlo = (wr - whi.astype(F32)).astype(BF16)
    br = jnp.concatenate([bg, be, jnp.zeros((npad,), F32)]).reshape(1, LANES)
    hp, rp = _router(yp, mod_p[3], mod_p[4], whi, wlo, br)
    hs, rs = _router(ys, mod_s[3], mod_s[4], whi, wlo, br)
    h = jnp.concatenate([hp, hs], 0)
    route = jnp.concatenate([rp, rs], 0)
    order, inv, items = _dispatch(route[:, :MOE_TOPK].astype(jnp.int32))
    yb = _experts(h[order // MOE_TOPK], items, w1, w3, w2, layer)
    y0, y1 = yb[inv[:, 0]], yb[inv[:, 1]]
    yp = _combine_ln(y0, y1, route, 0, yp, mod_p[5], g, b, alpha)
    ys = _combine_ln(y0, y1, route, hp.shape[0], ys, mod_s[5], g, b, alpha)
    return yp, ys


def kernel(x_prompt, x_sample, cache_diff_k, cache_diff_v, cache_dil_kv_g0, cache_dil_kv_g1, cache_dil_kv_g2, cache_mla_ckv, cache_mla_krope, page_table, c_prompt, c_sample, rel_bias, ada_w, ada_b, ln_g, ln_b, diff_w_in, diff_lam, diff_subln, diff_w_out, dil_w_in, dil_w_out, mla_w_in, mla_q_norm, mla_kv_norm, mla_w_uq, mla_w_uk, mla_w_uv, mla_w_out, moe_wg, moe_bg, moe_we, moe_be, moe_w1, moe_w3, moe_w2):
    depth = ada_w.shape[0]
    b, s, d = x_prompt.shape
    db, t, _ = x_sample.shape
    past = page_table.shape[1] * cache_diff_k.shape[2]
    alpha = (2 * depth) ** 0.25
    dil_cache = (cache_dil_kv_g0, cache_dil_kv_g1, cache_dil_kv_g2)
    bias_d = _bias_by_dist(rel_bias, max(s, past + t))

    mods = _ada(jnp.concatenate([c_prompt, c_sample], 0), ada_w, ada_b)
    yp, ys = x_prompt, x_sample
    dk_p, dv_p, dk_s, dv_s = [], [], [], []
    dl_p = [[] for _ in DIL_PATTERNS]
    dl_s = [[] for _ in DIL_PATTERNS]
    ck_p, kr_p, ck_s, kr_s = [], [], [], []
    for i in range(depth):
        kind, j = i % N_MIXERS, i // N_MIXERS
        mod_p = [mods[i, :b, None, m * d:(m + 1) * d] for m in range(6)]
        mod_s = [mods[i, b:, None, m * d:(m + 1) * d] for m in range(6)]
        if kind == 0:
            nq = DIFF_HEADS * 2 * DIFF_DH
            nk = DIFF_KV_HEADS * 2 * DIFF_DH
            qkv = ((0, nq, 1.0), (nq, nq + nk, 1.0), (nq + nk, nq + 2 * nk, 1.0))
            splits_p = ((0, nq, DIFF_DH ** -0.5 * LOG2E),) + qkv[1:] + ((nq, nq + nk, 1.0),)
            w_in = diff_w_in[j].astype(BF16)
            lam_init = 0.8 - 0.6 * math.exp(-0.3 * i)
            lp = diff_lam[j].astype(F32)
            lam = jnp.exp(jnp.sum(lp[0] * lp[1])) - jnp.exp(jnp.sum(lp[2] * lp[3])) + lam_init
            qp, kp, vp, kbp, vtp = _proj(yp, mod_p[0], mod_p[1], w_in, splits_p, (BF16, F32, F32, BF16),
                                         wt=w_in[:, nq + nk:].T)
            qs, ks, vs = _proj(ys, mod_s[0], mod_s[1], w_in, qkv, (F32, F32, F32))
            o_p = _diff_prompt_attn(qp, kbp, vtp, bias_d, lam, lam_init, diff_subln[j], b, s)
            o_s = _diff_sample_attn(qs, ks, vs, cache_diff_k, cache_diff_v, j, page_table, bias_d, lam, lam_init,
                                    diff_subln[j], db, t)
            dk_p.append(kp.reshape(b, s, DIFF_KV_HEADS, 2 * DIFF_DH))
            dv_p.append(vp.reshape(b, s, DIFF_KV_HEADS, 2 * DIFF_DH))
            dk_s.append(ks.reshape(db, t, DIFF_KV_HEADS, 2 * DIFF_DH))
            dv_s.append(vs.reshape(db, t, DIFF_KV_HEADS, 2 * DIFF_DH))
            w_out = diff_w_out[j].astype(BF16)
            yp = _outproj_ln(o_p, yp, mod_p[2], w_out, ln_g[i, 0], ln_b[i, 0], alpha)
            ys = _outproj_ln(o_s, ys, mod_s[2], w_out, ln_g[i, 0], ln_b[i, 0], alpha)
        elif kind == 1:
            hw = DIL_HEADS * DIL_DH
            splits, dts_p, dts_s = [], [], []
            for g in range(len(DIL_PATTERNS)):
                splits += [(3 * g * hw, (3 * g + 1) * hw, 1.0), ((3 * g + 1) * hw, (3 * g + 3) * hw, 1.0)]
                dts_p += [BF16, F32]
                dts_s += [F32, F32]
            w_in = dil_w_in[j].astype(BF16)
            pp = _proj(yp, mod_p[0], mod_p[1], w_in, splits, dts_p)
            ps = _proj(ys, mod_s[0], mod_s[1], w_in, splits, dts_s)
            outs, lses = [], []
            for g, (win, dil) in enumerate(DIL_PATTERNS):
                o_g, lse_g = _dil_prompt_group(pp[2 * g], pp[2 * g + 1], bias_d, win, dil, b, s)
                outs.append(o_g)
                lses.append(lse_g)
                kv = pp[2 * g + 1].reshape(b, s, 2, DIL_HEADS, DIL_DH)
                dl_p[g].append(kv[:, s - min(win, s):])
            bufs = [cb[j] for cb in dil_cache]
            kv_news = [ps[2 * g + 1] for g in range(len(DIL_PATTERNS))]
            o_s = _dil_sample_attn([ps[2 * g] for g in range(len(DIL_PATTERNS))], kv_news, bufs, bias_d, past, db, t)
            for g in range(len(DIL_PATTERNS)):
                new = kv_news[g].reshape(db, t, 2, DIL_HEADS, DIL_DH).astype(bufs[g].dtype)
                dl_s[g].append(jnp.concatenate([bufs[g], new], 1)[:, -bufs[g].shape[1]:])
            w_out = dil_w_out[j].astype(BF16)
            yp = _merge_outproj_ln(outs, lses, yp, mod_p[2], w_out, ln_g[i, 0], ln_b[i, 0], alpha)
            ys = _outproj_ln(o_s, ys, mod_s[2], w_out, ln_g[i, 0], ln_b[i, 0], alpha)
        else:
            win, wqa, wqb, wk, wv, wabs = _mla_weights(mla_w_in[j], mla_w_uq[j], mla_w_uk[j], mla_w_uv[j])
            ckv_p, krp, qp, kfp, vtp = _mla_proj(yp, mod_p[0], mod_p[1], jnp.arange(s), win, mla_q_norm[j],
                                                 mla_kv_norm[j], wqa, wqb, wk, wv, True,
                                                 (MLA_NOPE + MLA_ROPE) ** -0.5 * LOG2E)
            ckv_s, krs, qs = _mla_proj(ys, mod_s[0], mod_s[1], past + jnp.arange(t), win, mla_q_norm[j],
                                       mla_kv_norm[j], wqa, wqb, wk, wv, False, 1.0)
            o_p = _mla_prompt_attn(qp, kfp, vtp, b, s)
            o_s = _mla_sample_attn(qs, ckv_s, krs, cache_mla_ckv, cache_mla_krope, j, page_table, wabs, wv, db, t)
            ck_p.append(ckv_p.reshape(b, s, MLA_KV_RANK))
            kr_p.append(krp.reshape(b, s, MLA_ROPE))
            ck_s.append(ckv_s.reshape(db, t, MLA_KV_RANK))
            kr_s.append(krs.reshape(db, t, MLA_ROPE))
            w_out = mla_w_out[j].astype(BF16)
            yp = _outproj_ln(o_p, yp, mod_p[2], w_out, ln_g[i, 0], ln_b[i, 0], alpha)
            ys = _outproj_ln(o_s, ys, mod_s[2], w_out, ln_g[i, 0], ln_b[i, 0], alpha)
        yp, ys = _moe_layer(yp, ys, mod_p, mod_s, ln_g[i, 1], ln_b[i, 1], moe_wg[i], moe_bg[i], moe_we[i],
                            moe_be[i], moe_w1, moe_w3, moe_w2, i, alpha)
    return (yp, ys, jnp.stack(dk_p), jnp.stack(dv_p), jnp.stack(dk_s), jnp.stack(dv_s),
            jnp.stack(dl_p[0]), jnp.stack(dl_p[1]), jnp.stack(dl_p[2]),
            jnp.stack(dl_s[0]), jnp.stack(dl_s[1]), jnp.stack(dl_s[2]),
            jnp.stack(ck_p), jnp.stack(kr_p), jnp.stack(ck_s), jnp.stack(kr_s))
```

```python
import functools
import math

import jax
import jax.numpy as jnp
from jax import lax
from jax.experimental import pallas as pl
from jax.experimental.pallas import tpu as pltpu

F32 = jnp.float32
BF16 = jnp.bfloat16

N_MIXERS = 3
N_BUCKETS = 32
REL_MAX_DIST = 2048
DIFF_DH = 64
DIFF_HEADS = 8
DIFF_KV_HEADS = 2
DIL_PATTERNS = ((128, 1), (512, 4), (2048, 16))
DIL_HEADS = 8
DIL_DH = 64
MLA_HEADS = 16
MLA_NOPE = 64
MLA_ROPE = 32
MLA_V = 64
MLA_Q_RANK = 384
MLA_KV_RANK = 256
ROPE_BASE = 10000.0
MOE_GROUPS = 4
MOE_EXPERTS_PER_GROUP = 8
MOE_EXPERTS = MOE_GROUPS * MOE_EXPERTS_PER_GROUP
MOE_TOPK = 2
LN_EPS = 1e-5
NEG_INF = -1e30
LOG2E = math.log2(math.e)

LANES = 128
ROW_TILE = 512
MOE_ROWS = 256
DIFF_TILE = 256
DIFF_KEY_TILES = 2
MLA_TILE = 512
MLA_HEADS_PER_STEP = 8
DIL_TILE = 128
PAGES_PER_STEP = 32
VMEM_LIMIT_BYTES = 56 * 1024 * 1024


def _params(*sem):
    return pltpu.CompilerParams(dimension_semantics=sem, vmem_limit_bytes=VMEM_LIMIT_BYTES)


def _dot(a, b):
    return jnp.dot(a, b, preferred_element_type=F32)


def _dot_nt(a, b):
    return lax.dot_general(a, b, (((1,), (1,)), ((), ())), preferred_element_type=F32)


def _resident(shape):
    nd = len(shape)
    return pl.BlockSpec(shape, lambda *_: (0,) * nd)


def _lane(shape, axis):
    return lax.broadcasted_iota(jnp.int32, shape, axis)


def _rel_bucket(dist):
    max_exact = N_BUCKETS // 2
    n = jnp.maximum(dist, 0)
    nf = jnp.maximum(n, 1).astype(F32)
    large = max_exact + (jnp.log(nf / max_exact) / math.log(REL_MAX_DIST / max_exact)
                         * (N_BUCKETS - max_exact)).astype(jnp.int32)
    return jnp.where(n < max_exact, n, jnp.minimum(large, N_BUCKETS - 1))


def _bias_by_dist(rel_bias, n):
    hit = _rel_bucket(jnp.arange(n))[None, :, None] == jnp.arange(N_BUCKETS)[None, None, :]
    return jnp.sum(jnp.where(hit, rel_bias.astype(F32).T[:, None, :], 0.0), -1)


def _toeplitz(w, rows, cols):
    p = rows + cols
    u = jnp.concatenate([w[..., :cols][..., ::-1], w[..., :1], w[..., cols:][..., ::-1]], -1)
    t = jnp.tile(u, (1,) * (w.ndim - 1) + (rows,))[..., :rows * (p - 1)]
    return t.reshape(w.shape[:-1] + (rows, p - 1))[..., :cols]


def _neg(shape):
    return jnp.full(shape, NEG_INF, F32)


def _rope_tables(pos):
    half = MLA_ROPE // 2
    inv = ROPE_BASE ** (-jnp.arange(half, dtype=F32) / half)
    ang = pos.astype(F32)[:, None] * inv[None, :]
    cos, sin = jnp.cos(ang), jnp.sin(ang)
    n = pos.shape[0]
    cq = jnp.concatenate([jnp.ones((n, MLA_NOPE), F32), cos, cos, jnp.zeros((n, 32), F32)], 1)
    sq = jnp.concatenate([jnp.zeros((n, MLA_NOPE), F32), -sin, sin, jnp.zeros((n, 32), F32)], 1)
    ck = jnp.concatenate([cos, cos, jnp.zeros((n, LANES - MLA_ROPE), F32)], 1)
    sk = jnp.concatenate([-sin, sin, jnp.zeros((n, LANES - MLA_ROPE), F32)], 1)
    return cq[None], sq[None], ck[None], sk[None]


def _ada_kernel(c_ref, w_ref, b_ref, o_ref):
    c = c_ref[...]
    a = (c * jax.nn.sigmoid(c)).astype(BF16)
    o_ref[...] = _dot(a, w_ref[...].astype(BF16)) + b_ref[...]


def _ada(c_all, ada_w, ada_b):
    depth, d, n = ada_w.shape
    m = c_all.shape[0]
    tn = 1536
    return pl.pallas_call(
        _ada_kernel,
        grid=(depth, n // tn),
        in_specs=[pl.BlockSpec((m, d), lambda l, j: (0, 0)),
                  pl.BlockSpec((None, d, tn), lambda l, j: (l, 0, j)),
                  pl.BlockSpec((None, 1, tn), lambda l, j: (l, 0, j))],
        out_specs=pl.BlockSpec((None, m, tn), lambda l, j: (l, 0, j)),
        out_shape=jax.ShapeDtypeStruct((depth, m, n), F32),
        compiler_params=_params("arbitrary", "arbitrary"),
        name="ada",
    )(c_all, ada_w, ada_b.reshape(depth, 1, n))


def _row_blocks(b, s):
    if s >= ROW_TILE:
        assert s % ROW_TILE == 0
        return 1, ROW_TILE
    tb = min(b, max(1, ROW_TILE // s))
    assert b % tb == 0
    return tb, s


def _x_spec(tb, ts, s, d):
    nst = s // ts
    return pl.BlockSpec((tb, ts, d), lambda i: (i // nst, i % nst, 0))


def _mod_spec(tb, ts, s, d):
    nst = s // ts
    return pl.BlockSpec((tb, 1, d), lambda i: (i // nst, 0, 0))


def _pos_spec(ts, s, c):
    nst = s // ts
    return pl.BlockSpec((1, ts, c), lambda i: (0, i % nst, 0))


def _rows_spec(rows, n, off=0):
    return pl.BlockSpec((rows, n), lambda i: (i + off, 0))


def _proj_kernel(x_ref, sh_ref, sc_ref, w_ref, *rest, splits, transposed):
    wt_ref = rest[0] if transposed else None
    o_refs = rest[1:] if transposed else rest
    tb, ts, d = x_ref.shape
    h = x_ref[...] * (1.0 + sc_ref[...]) + sh_ref[...]
    h = h.reshape(tb * ts, d).astype(BF16)
    for o_ref, (c0, c1, mul) in zip(o_refs, splits):
        for s in range(c0, c1, 512):
            e = min(s + 512, c1)
            o_ref[:, s - c0:e - c0] = (_dot(h, w_ref[:, s:e]) * mul).astype(o_ref.dtype)
    if transposed:
        o_refs[-1][...] = _dot_nt(wt_ref[...], h).astype(o_refs[-1].dtype)


def _proj(x, shift, scale, w, splits, dtypes, wt=None):
    b, s, d = x.shape
    tb, ts = _row_blocks(b, s)
    rows = tb * ts
    nst = s // ts
    in_specs = [_x_spec(tb, ts, s, d), _mod_spec(tb, ts, s, d), _mod_spec(tb, ts, s, d), _resident(w.shape)]
    out_specs = [_rows_spec(rows, c1 - c0) for (c0, c1, _) in splits]
    out_shape = [jax.ShapeDtypeStruct((b * s, c1 - c0), dt) for (c0, c1, _), dt in zip(splits, dtypes)]
    args = [x, shift, scale, w]
    if wt is not None:
        assert tb == 1
        in_specs.append(_resident(wt.shape))
        args.append(wt)
        out_specs.append(pl.BlockSpec((None, wt.shape[0], ts), lambda i: (i // nst, 0, i % nst)))
        out_shape.append(jax.ShapeDtypeStruct((b, wt.shape[0], s), BF16))
    return pl.pallas_call(
        functools.partial(_proj_kernel, splits=tuple(splits), transposed=wt is not None),
        grid=(b * s // rows,),
        in_specs=in_specs, out_specs=out_specs, out_shape=out_shape,
        compiler_params=_params("arbitrary"),
        name="proj",
    )(*args)


def _layernorm(z, g, b):
    mu = jnp.mean(z, -1, keepdims=True)
    zc = z - mu
    var = jnp.mean(zc * zc, -1, keepdims=True)
    return zc * lax.rsqrt(var + LN_EPS) * g + b


def _outproj_ln_kernel(o_ref, x_ref, gate_ref, w_ref, g_ref, b_ref, y_ref, *, alpha):
    tb, ts, d = x_ref.shape
    f = _dot(o_ref[...].astype(BF16), w_ref[...]).reshape(tb, ts, d)
    z = alpha * x_ref[...] + gate_ref[...] * f
    y_ref[...] = _layernorm(z, g_ref[...], b_ref[...])


def _outproj_ln(o2d, x, gate, w, g, b, alpha):
    bb, s, d = x.shape
    tb, ts = _row_blocks(bb, s)
    rows = tb * ts
    return pl.pallas_call(
        functools.partial(_outproj_ln_kernel, alpha=alpha),
        grid=(bb * s // rows,),
        in_specs=[_rows_spec(rows, o2d.shape[1]), _x_spec(tb, ts, s, d), _mod_spec(tb, ts, s, d),
                  _resident(w.shape), _resident((1, 1, d)), _resident((1, 1, d))],
        out_specs=_x_spec(tb, ts, s, d),
        out_shape=jax.ShapeDtypeStruct(x.shape, F32),
        compiler_params=_params("arbitrary"),
        name="outproj_ln",
    )(o2d, x, gate, w, g.reshape(1, 1, d), b.reshape(1, 1, d))


def _merge_outproj_ln_kernel(o0, o1, o2, l0, l1, l2, x_ref, gate_ref, w_ref, g_ref, b_ref, y_ref, *, alpha):
    tb, ts, d = x_ref.shape
    a0, a1, a2 = l0[...], l1[...], l2[...]
    mx = jnp.maximum(jnp.maximum(a0, a1), a2)
    e0, e1, e2 = jnp.exp(a0 - mx), jnp.exp(a1 - mx), jnp.exp(a2 - mx)
    o = (e0 * o0[...] + e1 * o1[...] + e2 * o2[...]) / (e0 + e1 + e2)
    f = _dot(o.astype(BF16), w_ref[...]).reshape(tb, ts, d)
    z = alpha * x_ref[...] + gate_ref[...] * f
    y_ref[...] = _layernorm(z, g_ref[...], b_ref[...])


def _merge_outproj_ln(outs, lses, x, gate, w, g, b, alpha):
    bb, s, d = x.shape
    tb, ts = _row_blocks(bb, s)
    rows = tb * ts
    k = outs[0].shape[1]
    return pl.pallas_call(
        functools.partial(_merge_outproj_ln_kernel, alpha=alpha),
        grid=(bb * s // rows,),
        in_specs=[_rows_spec(rows, k)] * 6 + [_x_spec(tb, ts, s, d), _mod_spec(tb, ts, s, d),
                                              _resident(w.shape), _resident((1, 1, d)), _resident((1, 1, d))],
        out_specs=_x_spec(tb, ts, s, d),
        out_shape=jax.ShapeDtypeStruct(x.shape, F32),
        compiler_params=_params("arbitrary"),
        name="merge_outproj_ln",
    )(*outs, *lses, x, gate, w, g.reshape(1, 1, d), b.reshape(1, 1, d))


def _softmax_step(s_blocks, vt_blocks, m_sc, l_sc, acc_sc, i):
    m_prev = m_sc[i]
    m_new = m_prev
    for s in s_blocks:
        m_new = jnp.maximum(m_new, jnp.max(s, 0, keepdims=True))
    a = jnp.exp2(m_prev - m_new)
    l = a * l_sc[i]
    acc = a * acc_sc[i]
    for s, v_t in zip(s_blocks, vt_blocks):
        p = jnp.exp2(s - m_new)
        l = l + jnp.sum(p, 0, keepdims=True)
        acc = acc + _dot(v_t, p.astype(BF16))
    l_sc[i] = l
    acc_sc[i] = acc
    m_sc[i] = m_new


def _causal_pairs(nt, kt=1):
    pairs = [(qi, ki) for qi in range(nt) for ki in range(qi // kt + 1)]
    return (jnp.asarray([p[0] for p in pairs], jnp.int32), jnp.asarray([p[1] for p in pairs], jnp.int32))


def _init_stats(m_sc, l_sc, acc_sc):
    m_sc[...] = jnp.full(m_sc.shape, NEG_INF, F32)
    l_sc[...] = jnp.zeros(l_sc.shape, F32)
    acc_sc[...] = jnp.zeros(acc_sc.shape, F32)


def _diff_prompt_kernel(qi_ref, ki_ref, lam_ref, q_ref, k_ref, vt_ref, bias_ref, subln_ref, o_ref, m_sc, l_sc, acc_sc,
                        *, post):
    pair = pl.program_id(2)
    qi, ki = qi_ref[pair], ki_ref[pair]
    r_heads = q_ref.shape[1] // LANES
    t = q_ref.shape[0]
    kt = k_ref.shape[0] // t

    @pl.when(ki == 0)
    def _():
        _init_stats(m_sc, l_sc, acc_sc)

    k = k_ref[...]
    v_ts = [vt_ref[:, j * t:(j + 1) * t] for j in range(kt)]
    first = _lane(k.shape, 1) < DIFF_DH
    rows = k.shape[0]
    k2 = jnp.concatenate([jnp.where(first, k, jnp.zeros_like(k)), jnp.where(first, jnp.zeros_like(k), k)], 0)
    scores = [_dot_nt(k2, q_ref[:, r * LANES:(r + 1) * LANES]) for r in range(r_heads)]
    for r in range(r_heads):
        biases = [bias_ref[r, jnp.maximum(qi - (ki * kt + j) + 1, 0)] for j in range(kt)]
        for mi in range(2):
            blocks = [scores[r][mi * rows + j * t:mi * rows + (j + 1) * t] + biases[j] for j in range(kt)]
            _softmax_step(blocks, v_ts, m_sc, l_sc, acc_sc, 2 * r + mi)

    @pl.when(ki == qi // kt)
    def _():
        lam = lam_ref[0, 0]
        for r in range(r_heads):
            o = acc_sc[2 * r] * (1.0 / l_sc[2 * r]) - lam * (acc_sc[2 * r + 1] * (1.0 / l_sc[2 * r + 1]))
            o = o * lax.rsqrt(jnp.mean(o * o, 0, keepdims=True) + LN_EPS) * subln_ref[...] * post
            o_ref[:, r * LANES:(r + 1) * LANES] = o.T.astype(o_ref.dtype)


def _diff_prompt_attn(q2d, kb2d, v_t, bias_d, lam, lam_init, subln, b, s):
    t = min(DIFF_TILE, s)
    nt = s // t
    r = DIFF_HEADS // DIFF_KV_HEADS
    gx = jnp.concatenate([_neg((DIFF_HEADS, t)), bias_d[:, :nt * t]], 1)
    win = jnp.stack([gx[:, d * t + 1:d * t + 2 * t] for d in range(nt)], 1)[..., ::-1]
    tiles = _toeplitz(win, t, t) * LOG2E
    tiles = jnp.concatenate([_neg((DIFF_HEADS, 1, t, t)), tiles], 1)
    hw = r * LANES
    kt = max(c for c in (1, 2, DIFF_KEY_TILES) if nt % c == 0)
    qis, kis = _causal_pairs(nt, kt)
    grid_spec = pltpu.PrefetchScalarGridSpec(
        num_scalar_prefetch=2,
        grid=(DIFF_KV_HEADS, b, qis.shape[0]),
        in_specs=[pl.BlockSpec(memory_space=pltpu.SMEM),
                  pl.BlockSpec((None, t, hw), lambda g, bb, p, qa, ka: (bb, qa[p], g)),
                  pl.BlockSpec((None, kt * t, LANES), lambda g, bb, p, qa, ka: (bb, ka[p], g)),
                  pl.BlockSpec((None, LANES, kt * t), lambda g, bb, p, qa, ka: (bb, g, ka[p])),
                  pl.BlockSpec((r, nt + 1, t, t), lambda g, bb, p, qa, ka: (g, 0, 0, 0)),
                  pl.BlockSpec((LANES, 1), lambda g, bb, p, qa, ka: (0, 0))],
        out_specs=pl.BlockSpec((None, t, hw), lambda g, bb, p, qa, ka: (bb, qa[p], g)),
        scratch_shapes=[pltpu.VMEM((2 * r, 1, t), F32), pltpu.VMEM((2 * r, 1, t), F32),
                        pltpu.VMEM((2 * r, LANES, t), F32)])
    return pl.pallas_call(
        functools.partial(_diff_prompt_kernel, post=1.0 - lam_init),
        grid_spec=grid_spec,
        out_shape=jax.ShapeDtypeStruct((b, s, DIFF_HEADS * LANES), BF16),
        compiler_params=_params("arbitrary", "arbitrary", "arbitrary"),
        name="diff_prompt_attn",
    )(qis, kis, lam.reshape(1, 1), q2d.reshape(b, s, -1), kb2d.reshape(b, s, -1), v_t, tiles,
      subln.reshape(LANES, 1)).reshape(b * s, -1)


def _as_column(row):
    n = row.shape[1]
    return jnp.sum(jnp.where(_lane((n, n), 0) == _lane((n, n), 1), row, 0.0), 1, keepdims=True)


def _decode_update(scores, vals, m_sc, l_sc, acc_sc):
    m_prev = m_sc[...]
    m_new = m_prev
    for s in scores:
        m_new = jnp.maximum(m_new, jnp.max(s, 0, keepdims=True))
    a = jnp.exp2(m_prev - m_new)
    l = a * l_sc[...]
    pv = None
    for s, v in zip(scores, vals):
        p = jnp.exp2(s - m_new)
        l = l + jnp.sum(p, 0, keepdims=True)
        part = _dot(p.T.astype(BF16), v)
        pv = part if pv is None else pv + part
    acc_sc[...] = _as_column(a) * acc_sc[...] + pv
    l_sc[...] = l
    m_sc[...] = m_new


def _diff_sample_kernel(pt_ref, lam_ref, q_ref, kn_ref, vn_ref, bias_ref, subln_ref, *rest, npg, qmul, post):
    k_refs, v_refs = rest[:npg], rest[npg:2 * npg]
    o_ref, qbd, m_sc, l_sc, acc_sc = rest[2 * npg:]
    c = pl.program_id(1)
    nc = pl.num_programs(1)
    t = q_ref.shape[0]

    @pl.when(c == 0)
    def _():
        q = q_ref[...] * qmul
        first = _lane((t, LANES), 1) < DIFF_DH
        zero = jnp.zeros((t, LANES), F32)
        heads = [q[:, h * LANES:(h + 1) * LANES] for h in range(DIFF_HEADS)]
        pieces = [jnp.where(first, qh, zero) for qh in heads] + [jnp.where(first, zero, qh) for qh in heads]
        qbd[...] = jnp.concatenate(pieces, 0).astype(BF16)
        _init_stats(m_sc, l_sc, acc_sc)

    def update(pages):
        scores = [_dot_nt(kp.astype(BF16), qbd[...]) + bias for kp, _, bias in pages]
        _decode_update(scores, [vp.astype(BF16) for _, vp, _ in pages], m_sc, l_sc, acc_sc)

    update([(k_refs[i][...], v_refs[i][...], bias_ref[c * npg + i]) for i in range(npg)])

    @pl.when(c == nc - 1)
    def _():
        pad = jnp.zeros((k_refs[0].shape[0] - kn_ref.shape[0], LANES), F32)
        update([(jnp.concatenate([kn_ref[...], pad], 0), jnp.concatenate([vn_ref[...], pad], 0),
                 bias_ref[nc * npg])])
        o = acc_sc[...] * _as_column(1.0 / l_sc[...])
        half = o.shape[0] // 2
        o = o[:half] - lam_ref[0, 0] * o[half:]
        o_ref[...] = o * lax.rsqrt(jnp.mean(o * o, -1, keepdims=True) + LN_EPS) * subln_ref[...] * post


def _diff_sample_attn(q2d, k2d, v2d, cache_k, cache_v, slot, page_table, bias_d, lam, lam_init, subln, db, t):
    n_phys, page = cache_k.shape[1], cache_k.shape[2]
    n_pages = page_table.shape[1]
    past = n_pages * page
    npg = min(PAGES_PER_STEP, n_pages)
    assert n_pages % npg == 0
    prow = page * DIFF_KV_HEADS
    ck = cache_k.reshape(cache_k.shape[0], n_phys, prow, LANES)
    cv = cache_v.reshape(cache_v.shape[0], n_phys, prow, LANES)
    cols = 2 * DIFF_HEADS * t
    assert cols == LANES
    npos = (n_pages + 1) * page
    f = _toeplitz(jnp.concatenate([_neg((DIFF_HEADS, page - 1)), bias_d[:, :past + t]], 1), t, npos)
    f = f.reshape(DIFF_HEADS, t, n_pages + 1, page).transpose(2, 3, 0, 1)
    own = (jnp.arange(DIFF_KV_HEADS)[:, None] == (jnp.arange(DIFF_HEADS) // (DIFF_HEADS // DIFF_KV_HEADS))[None, :])
    table = jnp.where(own[None, None, :, None, :, None], f[:, :, None, None, :, :] * LOG2E, NEG_INF)
    table = jnp.broadcast_to(table, (n_pages + 1, page, DIFF_KV_HEADS, 2, DIFF_HEADS, t))
    table = table.reshape(n_pages + 1, prow, cols)

    def page_spec(i):
        return pl.BlockSpec((None, None, prow, LANES), lambda b, c, pt: (slot, pt[b, c * npg + i], 0, 0))

    grid_spec = pltpu.PrefetchScalarGridSpec(
        num_scalar_prefetch=1,
        grid=(db, n_pages // npg),
        in_specs=[pl.BlockSpec(memory_space=pltpu.SMEM),
                  pl.BlockSpec((None, t, DIFF_HEADS * LANES), lambda b, c, pt: (b, 0, 0)),
                  pl.BlockSpec((None, t * DIFF_KV_HEADS, LANES), lambda b, c, pt: (b, 0, 0)),
                  pl.BlockSpec((None, t * DIFF_KV_HEADS, LANES), lambda b, c, pt: (b, 0, 0)),
                  pl.BlockSpec(table.shape, lambda b, c, pt: (0, 0, 0)),
                  pl.BlockSpec((1, LANES), lambda b, c, pt: (0, 0))]
        + [page_spec(i) for i in range(npg)] + [page_spec(i) for i in range(npg)],
        out_specs=pl.BlockSpec((None, cols // 2, LANES), lambda b, c, pt: (b, 0, 0)),
        scratch_shapes=[pltpu.VMEM((cols, LANES), BF16),
                        pltpu.VMEM((1, cols), F32), pltpu.VMEM((1, cols), F32), pltpu.VMEM((cols, LANES), F32)])
    o = pl.pallas_call(
        functools.partial(_diff_sample_kernel, npg=npg, qmul=DIFF_DH ** -0.5 * LOG2E, post=1.0 - lam_init),
        grid_spec=grid_spec,
        out_shape=jax.ShapeDtypeStruct((db, cols // 2, LANES), F32),
        compiler_params=_params("arbitrary", "arbitrary"),
        name="diff_sample_attn",
    )(page_table, lam.reshape(1, 1), q2d.reshape(db, t, -1), k2d.reshape(db, t * DIFF_KV_HEADS, LANES),
      v2d.reshape(db, t * DIFF_KV_HEADS, LANES), table, subln.reshape(1, LANES),
      *([ck] * npg), *([cv] * npg))
    return o.reshape(db, DIFF_HEADS, t, LANES).transpose(0, 2, 1, 3).reshape(db * t, DIFF_HEADS * LANES)


def _dil_prompt_kernel(q_ref, kp_ref, kc_ref, vp_ref, vc_ref, bias_ref, o_ref, lse_ref, *, scale):
    qi = pl.program_id(2)
    tq = q_ref.shape[0]
    q = q_ref[...]
    k = jnp.concatenate([kp_ref[...], kc_ref[...]], 0).astype(BF16)
    v = jnp.concatenate([vp_ref[...], vc_ref[...]], 0).astype(BF16)
    col = _lane((tq, 2 * tq), 1)
    prev_ok = (col >= tq) | (qi > 0)
    first = _lane((tq, LANES), 1) < DIL_DH
    for p in range(DIL_HEADS // 2):
        sl = slice(p * LANES, (p + 1) * LANES)
        qp, kp, vp = q[:, sl], k[:, sl], v[:, sl]
        outs, lses = [], []
        for a in range(2):
            qa = jnp.where(first, qp, jnp.zeros_like(qp)) if a == 0 else jnp.where(first, jnp.zeros_like(qp), qp)
            s = _dot_nt(qa, kp) * scale + bias_ref[2 * p + a]
            s = jnp.where(prev_ok, s, NEG_INF)
            m = jnp.max(s, -1, keepdims=True)
            e = jnp.exp(s - m)
            l = jnp.sum(e, -1, keepdims=True)
            outs.append(_dot(e.astype(BF16), vp) / l)
            lses.append(m + jnp.log(l))
        o_ref[:, sl] = jnp.where(first, outs[0], outs[1])
        lse_ref[:, sl] = jnp.where(first, jnp.broadcast_to(lses[0], (tq, LANES)),
                                   jnp.broadcast_to(lses[1], (tq, LANES)))


def _dil_prompt_group(q2d, kv2d, bias_d, window, dil, b, s):
    ln = s // dil
    band = window // dil
    tq = min(DIL_TILE, ln)
    assert band <= tq or ln == tq
    nb = ln // tq
    hw = DIL_HEADS * DIL_DH
    wvec = jnp.concatenate([_neg((DIL_HEADS, tq - 1)), bias_d[:, :band * dil + 1:dil],
                            _neg((DIL_HEADS, 2 * tq - band - 1))], 1)
    table = _toeplitz(wvec, tq, 2 * tq)
    qv = q2d.reshape(b, ln, dil * hw)
    kvv = kv2d.reshape(b, ln, dil * 2 * hw)

    def kv_spec(prev, val):
        return pl.BlockSpec((None, tq, hw),
                            lambda bb, c, qi: (bb, jnp.maximum(qi - 1, 0) if prev else qi, 2 * c + val))

    o, lse = pl.pallas_call(
        functools.partial(_dil_prompt_kernel, scale=DIL_DH ** -0.5),
        grid=(b, dil, nb),
        in_specs=[pl.BlockSpec((None, tq, hw), lambda bb, c, qi: (bb, qi, c)),
                  kv_spec(True, 0), kv_spec(False, 0), kv_spec(True, 1), kv_spec(False, 1),
                  pl.BlockSpec(table.shape, lambda bb, c, qi: (0, 0, 0))],
        out_specs=[pl.BlockSpec((None, tq, hw), lambda bb, c, qi: (bb, qi, c))] * 2,
        out_shape=[jax.ShapeDtypeStruct((b, ln, dil * hw), F32)] * 2,
        compiler_params=_params("arbitrary", "arbitrary", "arbitrary"),
        name="dil_prompt_attn",
    )(qv, kvv, kvv, kvv, kvv, table)
    return o.reshape(b * s, hw), lse.reshape(b * s, hw)


def _dil_sample_kernel(q0, q1, q2, c0, c1, c2, n0, n1, n2, b0, b1, b2, o_ref, *, scale):
    t = q0.shape[0]
    hw = DIL_HEADS * DIL_DH
    rows = DIL_HEADS * t
    diag = (_lane((rows, hw), 1) // DIL_DH) == (_lane((rows, hw), 0) // t)
    outs, lses = [], []
    for q_ref, c_ref, n_ref, b_ref in ((q0, c0, n0, b0), (q1, c1, n1, b1), (q2, c2, n2, b2)):
        wb = c_ref.shape[-1]
        q = q_ref[...]
        qbd = jnp.where(diag, jnp.concatenate([q] * DIL_HEADS, 0), 0.0).astype(BF16)
        k_t = c_ref[0].reshape(hw, wb).astype(BF16)
        v_t = c_ref[1].reshape(hw, wb).astype(BF16)
        s_old = _dot(qbd, k_t) * scale + b_ref[:, :wb]
        s_new = _dot_nt(qbd, n_ref[:, :hw]) * scale + b_ref[:, wb:]
        m = jnp.maximum(jnp.max(s_old, -1, keepdims=True), jnp.max(s_new, -1, keepdims=True))
        e_old = jnp.exp(s_old - m)
        e_new = jnp.exp(s_new - m)
        l = jnp.sum(e_old, -1, keepdims=True) + jnp.sum(e_new, -1, keepdims=True)
        outs.append((_dot_nt(e_old.astype(BF16), v_t) + _dot(e_new.astype(BF16), n_ref[:, hw:])) / l)
        lses.append(m + jnp.log(l))
    mx = jnp.maximum(jnp.maximum(lses[0], lses[1]), lses[2])
    es = [jnp.exp(x - mx) for x in lses]
    o = (es[0] * outs[0] + es[1] * outs[1] + es[2] * outs[2]) / (es[0] + es[1] + es[2])
    o = jnp.where(diag, o, 0.0)
    o_ref[...] = jnp.sum(o.reshape(DIL_HEADS, t, hw), 0)


def _dil_sample_attn(qs, kv_news, bufs, bias_d, past, db, t):
    hw = DIL_HEADS * DIL_DH
    rows = DIL_HEADS * t
    caches, news, tables = [], [], []
    for (window, dil), kvn, buf in zip(DIL_PATTERNS, kv_news, bufs):
        wb = buf.shape[1]
        assert wb % LANES == 0
        caches.append(buf.transpose(0, 2, 3, 4, 1))
        news.append(jnp.concatenate([kvn.reshape(db, t, 2 * hw).astype(BF16),
                                     jnp.zeros((db, LANES - t, 2 * hw), BF16)], 1))
        wp = wb + LANES
        f = _toeplitz(jnp.concatenate([_neg((DIL_HEADS, wp - 1 - wb)), bias_d[:, :wb + t]], 1), t, wp)
        delta = wb + jnp.arange(t)[:, None] - jnp.arange(wp)[None, :]
        ok = (delta >= 0) & (delta % dil == 0) & (delta <= window) & (past - delta + jnp.arange(t)[:, None] >= 0)
        tables.append(jnp.where(ok[None], f, NEG_INF).reshape(rows, wp))
    o = pl.pallas_call(
        functools.partial(_dil_sample_kernel, scale=DIL_DH ** -0.5),
        grid=(db,),
        in_specs=[pl.BlockSpec((None, t, hw), lambda b: (b, 0, 0))] * 3
        + [pl.BlockSpec((None,) + c.shape[1:], lambda b: (b, 0, 0, 0, 0)) for c in caches]
        + [pl.BlockSpec((None, LANES, 2 * hw), lambda b: (b, 0, 0))] * 3
        + [_resident(tb.shape) for tb in tables],
        out_specs=pl.BlockSpec((None, t, hw), lambda b: (b, 0, 0)),
        out_shape=jax.ShapeDtypeStruct((db, t, hw), F32),
        compiler_params=_params("arbitrary"),
        name="dil_sample_attn",
    )(*[q.reshape(db, t, hw) for q in qs], *caches, *news, *tables)
    return o.reshape(db * t, hw)


def _mla_proj_kernel(x_ref, sh_ref, sc_ref, cq_ref, sq_ref, ck_ref, sk_ref, win_ref, qn_ref, kvn_ref,
                     wqa_ref, wqb_ref, *rest, with_kv, qmul):
    if with_kv:
        wk_ref, wv_ref, ckv_ref, kr_ref, q_ref, k_ref, v_ref = rest
    else:
        ckv_ref, kr_ref, q_ref = rest
    tb, ts, d = x_ref.shape
    rows = tb * ts
    h = x_ref[...] * (1.0 + sc_ref[...]) + sh_ref[...]
    h = h.reshape(rows, d).astype(BF16)
    proj = _dot(h, win_ref[...])
    r0, r1 = MLA_Q_RANK, MLA_Q_RANK + MLA_KV_RANK

    def rms(z, g):
        return z * lax.rsqrt(jnp.mean(z * z, -1, keepdims=True) + LN_EPS) * g

    cq = rms(proj[:, :r0], qn_ref[...]).astype(BF16)
    ckv = rms(proj[:, r0:r1], kvn_ref[...])
    kra = proj[:, r1:r1 + LANES].reshape(tb, ts, LANES)
    krb = proj[:, r1 + LANES:r1 + 2 * LANES].reshape(tb, ts, LANES)
    kr = (kra * ck_ref[...] + krb * sk_ref[...]).reshape(rows, LANES)
    ckv_ref[...] = ckv
    kr_ref[...] = kr[:, :MLA_ROPE]
    nq = q_ref.shape[1]
    cqt = jnp.concatenate([cq_ref[...]] * MLA_HEADS, -1)
    sqt = jnp.concatenate([sq_ref[...]] * MLA_HEADS, -1)
    qa = _dot(cq, wqa_ref[...]).reshape(tb, ts, nq)
    qb = _dot(cq, wqb_ref[...]).reshape(tb, ts, nq)
    q_ref[...] = ((qa * cqt + qb * sqt) * qmul).reshape(rows, nq).astype(q_ref.dtype)
    if with_kv:
        kcat = jnp.concatenate([ckv.astype(BF16), kr.astype(BF16)], 1)
        k_ref[...] = _dot(kcat, wk_ref[...]).astype(k_ref.dtype)
        v_ref[...] = _dot_nt(wv_ref[...], ckv.astype(BF16)).astype(v_ref.dtype)


def _mla_weights(w_in, w_uq, w_uk, w_uv):
    d = w_in.shape[0]
    r0, r1 = MLA_Q_RANK, MLA_Q_RANK + MLA_KV_RANK
    half = MLA_ROPE // 2
    zpad = jnp.zeros((d, LANES - MLA_ROPE), F32)
    kr_w = w_in[:, r1:]
    kr_sw = jnp.concatenate([kr_w[:, half:], kr_w[:, :half]], 1)
    win = jnp.concatenate([w_in[:, :r1], kr_w, zpad, kr_sw, zpad], 1).astype(BF16)
    hd = MLA_NOPE + MLA_ROPE
    wq = w_uq.reshape(MLA_Q_RANK, MLA_HEADS, hd)
    z32 = jnp.zeros((MLA_Q_RANK, MLA_HEADS, LANES - hd), F32)
    z64 = jnp.zeros((MLA_Q_RANK, MLA_HEADS, MLA_NOPE), F32)
    wqa = jnp.concatenate([wq, z32], -1).reshape(MLA_Q_RANK, MLA_HEADS * LANES).astype(BF16)
    wqb = jnp.concatenate([z64, wq[..., MLA_NOPE + half:], wq[..., MLA_NOPE:MLA_NOPE + half], z32], -1)
    wqb = wqb.reshape(MLA_Q_RANK, MLA_HEADS * LANES).astype(BF16)
    top = jnp.concatenate([w_uk, jnp.zeros((MLA_KV_RANK, MLA_HEADS, LANES - MLA_NOPE), F32)], -1)
    eye = jnp.eye(MLA_ROPE, dtype=F32)[:, None, :]
    mid = jnp.concatenate([jnp.zeros((MLA_ROPE, MLA_HEADS, MLA_NOPE), F32),
                           jnp.broadcast_to(eye, (MLA_ROPE, MLA_HEADS, MLA_ROPE)),
                           jnp.zeros((MLA_ROPE, MLA_HEADS, LANES - hd), F32)], -1)
    bot = jnp.zeros((LANES - MLA_ROPE, MLA_HEADS, LANES), F32)
    wk = jnp.concatenate([top, mid, bot], 0).reshape(MLA_KV_RANK + LANES, MLA_HEADS * LANES).astype(BF16)
    wv = w_uv.reshape(MLA_KV_RANK, MLA_HEADS * MLA_V).astype(BF16)
    a_top = jnp.concatenate([w_uk.transpose(1, 2, 0),
                             jnp.zeros((MLA_HEADS, MLA_NOPE, LANES), F32)], -1)
    a_mid = jnp.concatenate([jnp.zeros((MLA_ROPE, MLA_KV_RANK), F32), jnp.eye(MLA_ROPE, dtype=F32),
                             jnp.zeros((MLA_ROPE, LANES - MLA_ROPE), F32)], -1)
    a_mid = jnp.broadcast_to(a_mid[None], (MLA_HEADS, MLA_ROPE, MLA_KV_RANK + LANES))
    a_bot = jnp.zeros((MLA_HEADS, LANES - hd, MLA_KV_RANK + LANES), F32)
    wabs = jnp.concatenate([a_top, a_mid, a_bot], 1).reshape(MLA_HEADS * LANES, MLA_KV_RANK + LANES).astype(BF16)
    return win, wqa, wqb, wk, wv, wabs


def _mla_proj(x, shift, scale, pos, win, qn, kvn, wqa, wqb, wk, wv, with_kv, qmul):
    b, s, d = x.shape
    tb, ts = _row_blocks(b, s)
    rows = tb * ts
    cq, sq, ck, sk = _rope_tables(pos)
    nq = MLA_HEADS * LANES
    in_specs = [_x_spec(tb, ts, s, d), _mod_spec(tb, ts, s, d), _mod_spec(tb, ts, s, d),
                _pos_spec(ts, s, LANES), _pos_spec(ts, s, LANES), _pos_spec(ts, s, LANES), _pos_spec(ts, s, LANES),
                _resident(win.shape), _resident((1, MLA_Q_RANK)), _resident((1, MLA_KV_RANK)),
                _resident(wqa.shape), _resident(wqb.shape)]
    args = [x, shift, scale, cq, sq, ck, sk, win, qn.reshape(1, -1), kvn.reshape(1, -1), wqa, wqb]
    out_specs = [_rows_spec(rows, MLA_KV_RANK), _rows_spec(rows, MLA_ROPE), _rows_spec(rows, nq)]
    out_shape = [jax.ShapeDtypeStruct((b * s, MLA_KV_RANK), F32), jax.ShapeDtypeStruct((b * s, MLA_ROPE), F32),
                 jax.ShapeDtypeStruct((b * s, nq), BF16 if with_kv else F32)]
    if with_kv:
        assert tb == 1
        nst = s // ts
        nv = MLA_HEADS * MLA_V
        wv_t = wv.T
        in_specs += [_resident(wk.shape), _resident(wv_t.shape)]
        args += [wk, wv_t]
        out_specs += [_rows_spec(rows, nq), pl.BlockSpec((None, nv, ts), lambda i: (i // nst, 0, i % nst))]
        out_shape += [jax.ShapeDtypeStruct((b * s, nq), BF16), jax.ShapeDtypeStruct((b, nv, s), BF16)]
    return pl.pallas_call(
        functools.partial(_mla_proj_kernel, with_kv=with_kv, qmul=qmul),
        grid=(b * s // rows,),
        in_specs=in_specs, out_specs=out_specs, out_shape=out_shape,
        compiler_params=_params("arbitrary"),
        name="mla_proj",
    )(*args)


def _mla_prompt_kernel(qi_ref, ki_ref, q_ref, k_ref, vt_ref, o_ref, m_sc, l_sc, acc_sc):
    pair = pl.program_id(2)
    qi, ki = qi_ref[pair], ki_ref[pair]
    t = q_ref.shape[0]
    n_heads = q_ref.shape[1] // LANES

    @pl.when(ki == 0)
    def _():
        _init_stats(m_sc, l_sc, acc_sc)

    def update(diagonal):
        scores = [_dot_nt(k_ref[:, a * LANES:(a + 1) * LANES], q_ref[:, a * LANES:(a + 1) * LANES])
                  for a in range(n_heads)]
        for a in range(n_heads):
            s = scores[a]
            if diagonal:
                s = jnp.where(_lane((t, t), 0) <= _lane((t, t), 1), s, NEG_INF)
            _softmax_step([s], [vt_ref[(a // 2) * LANES:(a // 2 + 1) * LANES, :]], m_sc, l_sc, acc_sc, a)

    pl.when(ki < qi)(functools.partial(update, False))
    pl.when(ki == qi)(functools.partial(update, True))

    @pl.when(ki == qi)
    def _():
        first = _lane((LANES, t), 0) < MLA_V
        for p in range(n_heads // 2):
            o = jnp.where(first, acc_sc[2 * p] * (1.0 / l_sc[2 * p]), acc_sc[2 * p + 1] * (1.0 / l_sc[2 * p + 1]))
            o_ref[:, p * LANES:(p + 1) * LANES] = o.T.astype(o_ref.dtype)


def _mla_prompt_attn(q2d, k2d, v_t, b, s):
    t = min(MLA_TILE, s)
    nt = s // t
    hs = MLA_HEADS_PER_STEP
    qis, kis = _causal_pairs(nt)
    grid_spec = pltpu.PrefetchScalarGridSpec(
        num_scalar_prefetch=2,
        grid=(b, MLA_HEADS // hs, qis.shape[0]),
        in_specs=[pl.BlockSpec((None, t, hs * LANES), lambda bb, h, p, qa, ka: (bb, qa[p], h)),
                  pl.BlockSpec((None, t, hs * LANES), lambda bb, h, p, qa, ka: (bb, ka[p], h)),
                  pl.BlockSpec((None, hs * MLA_V, t), lambda bb, h, p, qa, ka: (bb, h, ka[p]))],
        out_specs=pl.BlockSpec((None, t, hs * MLA_V), lambda bb, h, p, qa, ka: (bb, qa[p], h)),
        scratch_shapes=[pltpu.VMEM((hs, 1, t), F32), pltpu.VMEM((hs, 1, t), F32), pltpu.VMEM((hs, LANES, t), F32)])
    return pl.pallas_call(
        _mla_prompt_kernel,
        grid_spec=grid_spec,
        out_shape=jax.ShapeDtypeStruct((b, s, MLA_HEADS * MLA_V), BF16),
        compiler_params=_params("arbitrary", "arbitrary", "arbitrary"),
        name="mla_prompt_attn",
    )(qis, kis, q2d.reshape(b, s, -1), k2d.reshape(b, s, -1), v_t).reshape(b * s, -1)


def _mla_sample_kernel(pt_ref, q_ref, cn_ref, rn_ref, wabs_ref, wv_ref, *rest, npg, qmul):
    c_refs, r_refs = rest[:npg], rest[npg:2 * npg]
    o_ref, qabs, rpad, rnew, m_sc, l_sc, acc_sc = rest[2 * npg:]
    c = pl.program_id(1)
    nc = pl.num_programs(1)
    t = q_ref.shape[0]
    cols = MLA_HEADS * t
    page = c_refs[0].shape[0]

    @pl.when(c == 0)
    def _():
        q = q_ref[...]
        qs = jnp.concatenate([q[:, h * LANES:(h + 1) * LANES] for h in range(MLA_HEADS)], 0)
        rh = _lane((cols, LANES), 0) // t
        qbd = jnp.concatenate([jnp.where(rh == h, qs, 0.0) for h in range(MLA_HEADS)], 1).astype(BF16)
        qabs[...] = (_dot(qbd, wabs_ref[...]) * qmul).astype(BF16)
        rpad[...] = jnp.zeros(rpad.shape, F32)
        rnew[...] = jnp.zeros(rnew.shape, F32)
        _init_stats(m_sc, l_sc, acc_sc)

    def update(lats, valid):
        scores, latbs = [], []
        for lat, rot in lats:
            latb = lat.astype(BF16)
            s = _dot_nt(jnp.concatenate([latb, rot.astype(BF16)], 1), qabs[...])
            scores.append(s if valid is None else jnp.where(valid, s, NEG_INF))
            latbs.append(latb)
        _decode_update(scores, latbs, m_sc, l_sc, acc_sc)

    for i in range(npg):
        rpad[i, :MLA_ROPE, :] = r_refs[i][...]
    update([(c_refs[i][...], rpad[i].T) for i in range(npg)], None)

    @pl.when(c == nc - 1)
    def _():
        rnew[:t, :MLA_ROPE] = rn_ref[...]
        lat = jnp.concatenate([cn_ref[...], jnp.zeros((page - t, MLA_KV_RANK), F32)], 0)
        update([(lat, rnew[...])], _lane((page, cols), 0) <= (_lane((page, cols), 1) % t))
        o_lat = (acc_sc[...] * _as_column(1.0 / l_sc[...])).astype(BF16)
        of = _dot(o_lat, wv_ref[...])
        n = of.shape[1]
        own = (_lane((cols, n), 1) // MLA_V) == (_lane((cols, n), 0) // t)
        o_ref[...] = jnp.sum(jnp.where(own, of, 0.0).reshape(MLA_HEADS, t, n), 0)


def _mla_sample_attn(q2d, ckv_new, kr_new, cache_ckv, cache_krope, slot, page_table, wabs, wv, db, t):
    page = cache_ckv.shape[2]
    n_pages = page_table.shape[1]
    npg = min(PAGES_PER_STEP, n_pages)
    assert n_pages % npg == 0
    cols = MLA_HEADS * t
    nv = MLA_HEADS * MLA_V

    assert page == LANES
    krope_t = cache_krope.transpose(0, 1, 3, 2)

    def page_spec(w, i):
        return pl.BlockSpec((None, None, page, w), lambda b, c, pt: (slot, pt[b, c * npg + i], 0, 0))

    def rope_spec(i):
        return pl.BlockSpec((None, None, MLA_ROPE, page), lambda b, c, pt: (slot, pt[b, c * npg + i], 0, 0))

    grid_spec = pltpu.PrefetchScalarGridSpec(
        num_scalar_prefetch=1,
        grid=(db, n_pages // npg),
        in_specs=[pl.BlockSpec((None, t, MLA_HEADS * LANES), lambda b, c, pt: (b, 0, 0)),
                  pl.BlockSpec((None, t, MLA_KV_RANK), lambda b, c, pt: (b, 0, 0)),
                  pl.BlockSpec((None, t, MLA_ROPE), lambda b, c, pt: (b, 0, 0)),
                  pl.BlockSpec(wabs.shape, lambda b, c, pt: (0, 0)),
                  pl.BlockSpec(wv.shape, lambda b, c, pt: (0, 0))]
        + [page_spec(MLA_KV_RANK, i) for i in range(npg)] + [rope_spec(i) for i in range(npg)],
        out_specs=pl.BlockSpec((None, t, nv), lambda b, c, pt: (b, 0, 0)),
        scratch_shapes=[pltpu.VMEM((cols, MLA_KV_RANK + LANES), BF16), pltpu.VMEM((npg, LANES, page), F32),
                        pltpu.VMEM((page, LANES), F32),
                        pltpu.VMEM((1, cols), F32), pltpu.VMEM((1, cols), F32),
                        pltpu.VMEM((cols, MLA_KV_RANK), F32)])
    o = pl.pallas_call(
        functools.partial(_mla_sample_kernel, npg=npg, qmul=(MLA_NOPE + MLA_ROPE) ** -0.5 * LOG2E),
        grid_spec=grid_spec,
        out_shape=jax.ShapeDtypeStruct((db, t, nv), F32),
        compiler_params=_params("arbitrary", "arbitrary"),
        name="mla_sample_attn",
    )(page_table, q2d.reshape(db, t, -1), ckv_new.reshape(db, t, -1), kr_new.reshape(db, t, -1), wabs, wv,
      *([cache_ckv] * npg), *([krope_t] * npg))
    return o.reshape(db * t, nv)


def _router_kernel(x_ref, sh_ref, sc_ref, whi_ref, wlo_ref, b_ref, h_ref, lg_ref):
    tb, ts, d = x_ref.shape
    h = (x_ref[...] * (1.0 + sc_ref[...]) + sh_ref[...]).reshape(tb * ts, d)
    hi = h.astype(BF16)
    lo = (h - hi.astype(F32)).astype(BF16)
    h_ref[...] = hi
    lg = _dot(hi, whi_ref[...]) + _dot(hi, wlo_ref[...]) + _dot(lo, whi_ref[...]) + b_ref[...]
    lane = _lane(lg.shape, 1)
    far = jnp.int32(LANES)

    def top(vals):
        v = jnp.max(vals, -1, keepdims=True)
        return v, jnp.min(jnp.where(vals == v, lane, far), -1, keepdims=True)

    is_group = lane < MOE_GROUPS
    gmax, gsel = top(jnp.where(is_group, lg, NEG_INF))
    gprob = 1.0 / jnp.sum(jnp.where(is_group, jnp.exp(lg - gmax), 0.0), -1, keepdims=True)
    lo_lane = MOE_GROUPS + MOE_EXPERTS_PER_GROUP * gsel
    el = jnp.where((lane >= lo_lane) & (lane < lo_lane + MOE_EXPERTS_PER_GROUP), lg, NEG_INF)
    v1, i1 = top(el)
    v2, i2 = top(jnp.where(lane == i1, NEG_INF, el))
    ex = jnp.exp(v2 - v1)
    g1 = gprob / (1.0 + ex)
    g2 = g1 * ex
    e1 = (i1 - MOE_GROUPS).astype(F32)
    e2 = (i2 - MOE_GROUPS).astype(F32)
    lg_ref[...] = jnp.where(lane == 0, e1, jnp.where(lane == 1, e2, jnp.where(lane == 2, g1, jnp.where(lane == 3, g2, 0.0))))


def _router(x, shift, scale, whi, wlo, bias):
    b, s, d = x.shape
    tb, ts = _row_blocks(b, s)
    rows = tb * ts
    return pl.pallas_call(
        _router_kernel,
        grid=(b * s // rows,),
        in_specs=[_x_spec(tb, ts, s, d), _mod_spec(tb, ts, s, d), _mod_spec(tb, ts, s, d),
                  _resident(whi.shape), _resident(wlo.shape), _resident(bias.shape)],
        out_specs=[_rows_spec(rows, d), _rows_spec(rows, LANES)],
        out_shape=[jax.ShapeDtypeStruct((b * s, d), BF16), jax.ShapeDtypeStruct((b * s, LANES), F32)],
        compiler_params=_params("arbitrary"),
        name="router",
    )(x, shift, scale, whi, wlo, bias)


def _expert_kernel(blk_ref, exp_ref, first_ref, fresh_ref, lo_ref, hi_ref, n_ref, x_ref, w1_ref, w3_ref, w2_ref,
                   y_ref, w1_sc, w3_sc, w2_sc):
    w = pl.program_id(0)

    @pl.when(w < n_ref[0])
    def _():
        @pl.when(fresh_ref[w] == 1)
        def _():
            w1_sc[...] = w1_ref[...].astype(BF16)
            w3_sc[...] = w3_ref[...].astype(BF16)
            w2_sc[...] = w2_ref[...].astype(BF16)

        x = x_ref[...]
        a = _dot(x, w1_sc[...])
        u = (a * jax.nn.sigmoid(a)) * _dot(x, w3_sc[...])
        y = _dot(u.astype(BF16), w2_sc[...]).astype(y_ref.dtype)
        row = blk_ref[w] * x.shape[0] + _lane((x.shape[0], 1), 0)
        mine = (row >= lo_ref[w]) & (row < hi_ref[w])

        @pl.when(first_ref[w] == 1)
        def _():
            y_ref[...] = jnp.where(mine, y, jnp.zeros_like(y))

        @pl.when(first_ref[w] == 0)
        def _():
            y_ref[...] = jnp.where(mine, y, y_ref[...])


def _experts(xs, items, w1, w3, w2, layer):
    a, d = xs.shape
    ff = w1.shape[3]
    n_items = items[0].shape[0]
    grid_spec = pltpu.PrefetchScalarGridSpec(
        num_scalar_prefetch=7,
        grid=(n_items,),
        in_specs=[pl.BlockSpec((MOE_ROWS, d), lambda w, blk, ex, *_: (blk[w], 0)),
                  pl.BlockSpec((None, None, d, ff), lambda w, blk, ex, *_: (layer, ex[w], 0, 0)),
                  pl.BlockSpec((None, None, d, ff), lambda w, blk, ex, *_: (layer, ex[w], 0, 0)),
                  pl.BlockSpec((None, None, ff, d), lambda w, blk, ex, *_: (layer, ex[w], 0, 0))],
        out_specs=pl.BlockSpec((MOE_ROWS, d), lambda w, blk, ex, *_: (blk[w], 0)),
        scratch_shapes=[pltpu.VMEM((d, ff), BF16), pltpu.VMEM((d, ff), BF16), pltpu.VMEM((ff, d), BF16)])
    return pl.pallas_call(
        _expert_kernel,
        grid_spec=grid_spec,
        out_shape=jax.ShapeDtypeStruct((a, d), BF16),
        compiler_params=_params("arbitrary"),
        name="experts",
    )(*items, xs, w1, w3, w2)


def _combine_ln_kernel(y0_ref, y1_ref, r_ref, x_ref, gate_ref, g_ref, b_ref, o_ref, *, alpha):
    tb, ts, d = x_ref.shape
    r = r_ref[...]
    y = y0_ref[...].astype(F32) * r[:, 2:3] + y1_ref[...].astype(F32) * r[:, 3:4]
    z = alpha * x_ref[...] + gate_ref[...] * y.reshape(tb, ts, d)
    o_ref[...] = _layernorm(z, g_ref[...], b_ref[...])


def _combine_ln(y0, y1, route, row_off, x, gate, g, b, alpha):
    bb, s, d = x.shape
    tb, ts = _row_blocks(bb, s)
    rows = tb * ts
    assert row_off % rows == 0
    off = row_off // rows
    return pl.pallas_call(
        functools.partial(_combine_ln_kernel, alpha=alpha),
        grid=(bb * s // rows,),
        in_specs=[_rows_spec(rows, d, off), _rows_spec(rows, d, off), _rows_spec(rows, LANES, off),
                  _x_spec(tb, ts, s, d), _mod_spec(tb, ts, s, d), _resident((1, 1, d)), _resident((1, 1, d))],
        out_specs=_x_spec(tb, ts, s, d),
        out_shape=jax.ShapeDtypeStruct(x.shape, F32),
        compiler_params=_params("arbitrary"),
        name="combine_ln",
    )(y0, y1, route, x, gate, g.reshape(1, 1, d), b.reshape(1, 1, d))


def _dispatch(eid):
    n = eid.shape[0]
    a = n * MOE_TOPK
    n_blocks = -(-a // MOE_ROWS)
    flat_e = eid.reshape(a)
    order = jnp.argsort(flat_e).astype(jnp.int32)
    inv = jnp.argsort(order).astype(jnp.int32).reshape(n, MOE_TOPK)
    counts = jnp.sum((flat_e[:, None] == jnp.arange(MOE_EXPERTS)[None, :]).astype(jnp.int32), 0)
    end = jnp.cumsum(counts)
    start = end - counts
    b0 = start // MOE_ROWS
    nb = jnp.where(counts > 0, (end - 1) // MOE_ROWS - b0 + 1, 0)
    cum = jnp.cumsum(nb)
    n_items = n_blocks + MOE_EXPERTS - 1
    w = jnp.arange(n_items)
    ex = jnp.minimum(jnp.sum((cum[None, :] <= w[:, None]).astype(jnp.int32), 1), MOE_EXPERTS - 1)
    live = w < cum[-1]
    blk = jnp.where(live, b0[ex] + w - (cum - nb)[ex], n_blocks - 1)
    ex = jnp.where(live, ex, ex[jnp.maximum(cum[-1] - 1, 0)])
    prev_blk = jnp.concatenate([jnp.full((1,), -1, blk.dtype), blk[:-1]])
    prev_ex = jnp.concatenate([jnp.full((1,), -1, ex.dtype), ex[:-1]])
    items = [blk, ex, blk != prev_blk, ex != prev_ex, start[ex], end[ex], cum[-1:]]
    return order, inv, [x.astype(jnp.int32) for x in items]


def _moe_layer(yp, ys, mod_p, mod_s, g, b, wg, bg, we, be, w1, w3, w2, layer, alpha):
    d = yp.shape[-1]
    npad = LANES - MOE_GROUPS - MOE_EXPERTS
    wr = jnp.concatenate([wg, we, jnp.zeros((d, npad), F32)], 1)
    whi = wr.astype(BF16)
    wlo = (wr - whi.astype(F32)).astype(BF16)
    br = jnp.concatenate([bg, be, jnp.zeros((npad,), F32)]).reshape(1, LANES)
    hp, rp = _router(yp, mod_p[3], mod_p[4], whi, wlo, br)
    hs, rs = _router(ys, mod_s[3], mod_s[4], whi, wlo, br)
    h = jnp.concatenate([hp, hs], 0)
    route = jnp.concatenate([rp, rs], 0)
    order, inv, items = _dispatch(route[:, :MOE_TOPK].astype(jnp.int32))
    yb = _experts(h[order // MOE_TOPK], items, w1, w3, w2, layer)
    y0, y1 = yb[inv[:, 0]], yb[inv[:, 1]]
    yp = _combine_ln(y0, y1, route, 0, yp, mod_p[5], g, b, alpha)
    ys = _combine_ln(y0, y1, route, hp.shape[0], ys, mod_s[5], g, b, alpha)
    return yp, ys


def kernel(x_prompt, x_sample, cache_diff_k, cache_diff_v, cache_dil_kv_g0, cache_dil_kv_g1, cache_dil_kv_g2, cache_mla_ckv, cache_mla_krope, page_table, c_prompt, c_sample, rel_bias, ada_w, ada_b, ln_g, ln_b, diff_w_in, diff_lam, diff_subln, diff_w_out, dil_w_in, dil_w_out, mla_w_in, mla_q_norm, mla_kv_norm, mla_w_uq, mla_w_uk, mla_w_uv, mla_w_out, moe_wg, moe_bg, moe_we, moe_be, moe_w1, moe_w3, moe_w2):
    depth = ada_w.shape[0]
    b, s, d = x_prompt.shape
    db, t, _ = x_sample.shape
    past = page_table.shape[1] * cache_diff_k.shape[2]
    alpha = (2 * depth) ** 0.25
    dil_cache = (cache_dil_kv_g0, cache_dil_kv_g1, cache_dil_kv_g2)
    bias_d = _bias_by_dist(rel_bias, max(s, past + t))

    mods = _ada(jnp.concatenate([c_prompt, c_sample], 0), ada_w, ada_b)
    yp, ys = x_prompt, x_sample
    dk_p, dv_p, dk_s, dv_s = [], [], [], []
    dl_p = [[] for _ in DIL_PATTERNS]
    dl_s = [[] for _ in DIL_PATTERNS]
    ck_p, kr_p, ck_s, kr_s = [], [], [], []
    for i in range(depth):
        kind, j = i % N_MIXERS, i // N_MIXERS
        mod_p = [mods[i, :b, None, m * d:(m + 1) * d] for m in range(6)]
        mod_s = [mods[i, b:, None, m * d:(m + 1) * d] for m in range(6)]
        if kind == 0:
            nq = DIFF_HEADS * 2 * DIFF_DH
            nk = DIFF_KV_HEADS * 2 * DIFF_DH
            qkv = ((0, nq, 1.0), (nq, nq + nk, 1.0), (nq + nk, nq + 2 * nk, 1.0))
            splits_p = ((0, nq, DIFF_DH ** -0.5 * LOG2E),) + qkv[1:] + ((nq, nq + nk, 1.0),)
            w_in = diff_w_in[j].astype(BF16)
            lam_init = 0.8 - 0.6 * math.exp(-0.3 * i)
            lp = diff_lam[j].astype(F32)
            lam = jnp.exp(jnp.sum(lp[0] * lp[1])) - jnp.exp(jnp.sum(lp[2] * lp[3])) + lam_init
            qp, kp, vp, kbp, vtp = _proj(yp, mod_p[0], mod_p[1], w_in, splits_p, (BF16, F32, F32, BF16),
                                         wt=w_in[:, nq + nk:].T)
            qs, ks, vs = _proj(ys, mod_s[0], mod_s[1], w_in, qkv, (F32, F32, F32))
            o_p = _diff_prompt_attn(qp, kbp, vtp, bias_d, lam, lam_init, diff_subln[j], b, s)
            o_s = _diff_sample_attn(qs, ks, vs, cache_diff_k, cache_diff_v, j, page_table, bias_d, lam, lam_init,
                                    diff_subln[j], db, t)
            dk_p.append(kp.reshape(b, s, DIFF_KV_HEADS, 2 * DIFF_DH))
            dv_p.append(vp.reshape(b, s, DIFF_KV_HEADS, 2 * DIFF_DH))
            dk_s.append(ks.reshape(db, t, DIFF_KV_HEADS, 2 * DIFF_DH))
            dv_s.append(vs.reshape(db, t, DIFF_KV_HEADS, 2 * DIFF_DH))
            w_out = diff_w_out[j].astype(BF16)
            yp = _outproj_ln(o_p, yp, mod_p[2], w_out, ln_g[i, 0], ln_b[i, 0], alpha)
            ys = _outproj_ln(o_s, ys, mod_s[2], w_out, ln_g[i, 0], ln_b[i, 0], alpha)
        elif kind == 1:
            hw = DIL_HEADS * DIL_DH
            splits, dts_p, dts_s = [], [], []
            for g in range(len(DIL_PATTERNS)):
                splits += [(3 * g * hw, (3 * g + 1) * hw, 1.0), ((3 * g + 1) * hw, (3 * g + 3) * hw, 1.0)]
                dts_p += [BF16, F32]
                dts_s += [F32, F32]
            w_in = dil_w_in[j].astype(BF16)
            pp = _proj(yp, mod_p[0], mod_p[1], w_in, splits, dts_p)
            ps = _proj(ys, mod_s[0], mod_s[1], w_in, splits, dts_s)
            outs, lses = [], []
            for g, (win, dil) in enumerate(DIL_PATTERNS):
                o_g, lse_g = _dil_prompt_group(pp[2 * g], pp[2 * g + 1], bias_d, win, dil, b, s)
                outs.append(o_g)
                lses.append(lse_g)
                kv = pp[2 * g + 1].reshape(b, s, 2, DIL_HEADS, DIL_DH)
                dl_p[g].append(kv[:, s - min(win, s):])
            bufs = [cb[j] for cb in dil_cache]
            kv_news = [ps[2 * g + 1] for g in range(len(DIL_PATTERNS))]
            o_s = _dil_sample_attn([ps[2 * g] for g in range(len(DIL_PATTERNS))], kv_news, bufs, bias_d, past, db, t)
            for g in range(len(DIL_PATTERNS)):
                new = kv_news[g].reshape(db, t, 2, DIL_HEADS, DIL_DH).astype(bufs[g].dtype)
                dl_s[g].append(jnp.concatenate([bufs[g], new], 1)[:, -bufs[g].shape[1]:])
            w_out = dil_w_out[j].astype(BF16)
            yp = _merge_outproj_ln(outs, lses, yp, mod_p[2], w_out, ln_g[i, 0], ln_b[i, 0], alpha)
            ys = _outproj_ln(o_s, ys, mod_s[2], w_out, ln_g[i, 0], ln_b[i, 0], alpha)
        else:
            win, wqa, wqb, wk, wv, wabs = _mla_weights(mla_w_in[j], mla_w_uq[j], mla_w_uk[j], mla_w_uv[j])
            ckv_p, krp, qp, kfp, vtp = _mla_proj(yp, mod_p[0], mod_p[1], jnp.arange(s), win, mla_q_norm[j],
                                                 mla_kv_norm[j], wqa, wqb, wk, wv, True,
                                                 (MLA_NOPE + MLA_ROPE) ** -0.5 * LOG2E)
            ckv_s, krs, qs = _mla_proj(ys, mod_s[0], mod_s[1], past + jnp.arange(t), win, mla_q_norm[j],
                                       mla_kv_norm[j], wqa, wqb, wk, wv, False, 1.0)
            o_p = _mla_prompt_attn(qp, kfp, vtp, b, s)
            o_s = _mla_sample_attn(qs, ckv_s, krs, cache_mla_ckv, cache_mla_krope, j, page_table, wabs, wv, db, t)
            ck_p.append(ckv_p.reshape(b, s, MLA_KV_RANK))
            kr_p.append(krp.reshape(b, s, MLA_ROPE))
            ck_s.append(ckv_s.reshape(db, t, MLA_KV_RANK))
            kr_s.append(krs.reshape(db, t, MLA_ROPE))
            w_out = mla_w_out[j].astype(BF16)
            yp = _outproj_ln(o_p, yp, mod_p[2], w_out, ln_g[i, 0], ln_b[i, 0], alpha)
            ys = _outproj_ln(o_s, ys, mod_s[2], w_out, ln_g[i, 0], ln_b[i, 0], alpha)
        yp, ys = _moe_layer(yp, ys, mod_p, mod_s, ln_g[i, 1], ln_b[i, 1], moe_wg[i], moe_bg[i], moe_we[i],
                            moe_be[i], moe_w1, moe_w3, moe_w2, i, alpha)
    return (yp, ys, jnp.stack(dk_p), jnp.stack(dv_p), jnp.stack(dk_s), jnp.stack(dv_s),
            jnp.stack(dl_p[0]), jnp.stack(dl_p[1]), jnp.stack(dl_p[2]),
            jnp.stack(dl_s[0]), jnp.stack(dl_s[1]), jnp.stack(dl_s[2]),
            jnp.stack(ck_p), jnp.stack(kr_p), jnp.stack(ck_s), jnp.stack(kr_s))
```
